```python
import math
import jax
import jax.numpy as jnp
from jax import lax
import numpy as np

D_MODEL = 2048
BATCH = 2
SEQ = 4096
DEPTH = 2
DEC_BATCH = 128
DEC_SEQ = 1
PAST_LEN = 2048
PAGE_SIZE = 128

HEAD_DIM = 64
D_MIX = D_MODEL
D_A = D_MIX // 4
H_A = D_A // HEAD_DIM
D_B = (3 * D_MIX) // 8
H_B = D_B // HEAD_DIM
G_B = 4
N_B = 128
CONV_B = D_B + 2 * G_B * N_B
D_C = D_MIX - D_A - D_B
H_C = D_C // HEAD_DIM
CONV_W = 4
LRU_C = 8.0
SSD_CHUNK = 128
Q_BLOCK = 128
ATTN_SCALE = 1.0 / math.sqrt(HEAD_DIM)
PROJ_SPLITS = (D_A, D_A, D_B, CONV_B, H_B, D_C, D_C, D_C, H_C)
N_IN = sum(PROJ_SPLITS)
N_GROUPS = 4
E_PER_GROUP = 4
N_EXPERTS = N_GROUPS * E_PER_GROUP
TOP_K = 2
D_FF = D_MODEL // 4
EPS = 1e-6

kernel_name = 'hybrid_rglru_ssd_fox_hmoe_step'


def rmsnorm(x, g):
    xf = x.astype(jnp.float32)
    y = xf * lax.rsqrt(jnp.mean(xf * xf, axis=-1, keepdims=True) + EPS)
    return (y * g.astype(jnp.float32)).astype(x.dtype)


def causal_conv(x, prev, w, b):
    L = x.shape[1]
    xp = jnp.concatenate([prev.astype(x.dtype), x], axis=1)
    y = b + w[0] * xp[:, 0:L]
    for j in range(1, CONV_W):
        y = y + w[j] * xp[:, j:j + L]
    return y, xp[:, L:]


def split_proj(u):
    parts, start = [], 0
    for size in PROJ_SPLITS:
        parts.append(u[..., start:start + size])
        start += size
    return parts


def rglru_mixer(xa, ga, h0, conv_prev, conv_w, conv_b, w_a, b_a, w_x, b_x, lam):
    Bsz, L, _ = xa.shape
    xc, conv_new = causal_conv(xa, conv_prev, conv_w, conv_b)
    xh = xc.reshape(Bsz, L, H_A, HEAD_DIM)
    r = jax.nn.sigmoid(jnp.einsum('blhi,hij->blhj', xh, w_a).reshape(Bsz, L, D_A) + b_a)
    i = jax.nn.sigmoid(jnp.einsum('blhi,hij->blhj', xh, w_x).reshape(Bsz, L, D_A) + b_x)
    log_a = LRU_C * r.astype(jnp.float32) * jax.nn.log_sigmoid(lam.astype(jnp.float32))
    a = jnp.exp(log_a)
    u = jnp.sqrt(-jnp.expm1(2.0 * log_a)) * (i * xc).astype(jnp.float32)
    u = u.at[:, 0].add(a[:, 0] * h0.astype(jnp.float32))

    def combine(left, right):
        a_l, b_l = left
        a_r, b_r = right
        return a_l * a_r, a_r * b_l + b_r

    _, h = lax.associative_scan(combine, (a, u), axis=1)
    y = h.astype(xa.dtype) * jax.nn.gelu(ga)
    return y, h[:, -1].astype(h0.dtype), conv_new


def ssd_mixer(z, xbc, dt_raw, h0, conv_prev, conv_w, conv_b, dt_bias, a_log, d_skip, norm_g):
    Bsz, L, _ = xbc.shape
    R = H_B // G_B
    xbc, conv_new = causal_conv(xbc, conv_prev, conv_w, conv_b)
    xbc = jax.nn.silu(xbc)
    x = xbc[..., :D_B].reshape(Bsz, L, G_B, R, HEAD_DIM)
    bm = xbc[..., D_B:D_B + G_B * N_B].reshape(Bsz, L, G_B, N_B)
    cm = xbc[..., D_B + G_B * N_B:].reshape(Bsz, L, G_B, N_B)
    dt = jax.nn.softplus(dt_raw.astype(jnp.float32) + dt_bias.astype(jnp.float32)).reshape(Bsz, L, G_B, R)
    dt_a = dt * (-jnp.exp(a_log.astype(jnp.float32))).reshape(G_B, R)
    q = SSD_CHUNK if L % SSD_CHUNK == 0 else L
    nc = L // q

    def chunks(t):
        return t.reshape((Bsz, nc, q) + t.shape[2:])

    xc = chunks(x).astype(jnp.float32)
    bc = chunks(bm).astype(jnp.float32)
    cc = chunks(cm).astype(jnp.float32)
    dtc, dtac = chunks(dt), chunks(dt_a)
    cum = jnp.cumsum(dtac, axis=2)
    causal = jnp.tril(jnp.ones((q, q), dtype=bool))[None, None, :, :, None, None]
    seg = cum[:, :, :, None] - cum[:, :, None, :]
    decay = jnp.exp(jnp.where(causal, seg, -jnp.inf))
    scores = jnp.einsum('bclgn,bcsgn->bclsg', cc, bc)
    w_ls = scores[..., None] * decay * dtc[:, :, None]
    y_diag = jnp.einsum('bclsgr,bcsgrp->bclgrp', w_ls, xc)
    decay_end = jnp.exp(cum[:, :, -1:] - cum) * dtc
    chunk_states = jnp.einsum('bclgn,bclgr,bclgrp->bcgrpn', bc, decay_end, xc)
    chunk_decay = jnp.exp(cum[:, :, -1])

    def step(h, inp):
        s_c, d_c = inp
        return d_c[..., None, None] * h + s_c, h

    h_init = h0.astype(jnp.float32).reshape(Bsz, G_B, R, HEAD_DIM, N_B)
    h_last, h_start = lax.scan(step, h_init, (jnp.moveaxis(chunk_states, 1, 0), jnp.moveaxis(chunk_decay, 1, 0)))
    h_start = jnp.moveaxis(h_start, 0, 1)
    y_off = jnp.einsum('bclgn,bcgrpn,bclgr->bclgrp', cc, h_start, jnp.exp(cum))
    y = y_diag + y_off + d_skip.astype(jnp.float32).reshape(G_B, R, 1) * xc
    y = y.reshape(Bsz, L, D_B) * jax.nn.silu(z.astype(jnp.float32))
    y = rmsnorm(y.reshape(Bsz, L, G_B, D_B // G_B), norm_g.reshape(G_B, D_B // G_B))
    return y.reshape(Bsz, L, D_B).astype(z.dtype), h_last.reshape(h0.shape).astype(h0.dtype), conv_new


def fox_prompt(q, k, v, logf):
    Bsz, L = q.shape[:2]
    c = jnp.cumsum(logf, axis=1).transpose(0, 2, 1)
    nq = L // Q_BLOCK
    q_blocks = q.reshape(Bsz, nq, Q_BLOCK, H_C, HEAD_DIM).swapaxes(0, 1)
    key_pos = jnp.arange(L)

    def block(args):
        i, qb = args
        q_pos = i * Q_BLOCK + jnp.arange(Q_BLOCK)
        c_q = lax.dynamic_slice_in_dim(c, i * Q_BLOCK, Q_BLOCK, axis=2)
        s = jnp.einsum('bqhd,bkhd->bhqk', qb, k, preferred_element_type=jnp.float32) * ATTN_SCALE
        s = s + (c_q[..., None] - c[:, :, None, :])
        s = jnp.where(key_pos[None, :] <= q_pos[:, None], s, -jnp.inf)
        p = jax.nn.softmax(s, axis=-1)
        return jnp.einsum('bhqk,bkhd->bqhd', p.astype(v.dtype), v)

    o = lax.map(block, (jnp.arange(nq), q_blocks))
    return o.swapaxes(0, 1).reshape(Bsz, L, D_C)


def fox_sample(q, k, v, logf, k_past, v_past, logf_past):
    Bsz, L = q.shape[:2]
    P = k_past.shape[1]
    k_all = jnp.concatenate([k_past.astype(k.dtype), k], axis=1)
    v_all = jnp.concatenate([v_past.astype(v.dtype), v], axis=1)
    c = jnp.cumsum(jnp.concatenate([logf_past.astype(jnp.float32), logf], axis=1), axis=1).transpose(0, 2, 1)
    c_q = c[:, :, P:]
    s = jnp.einsum('bqhd,bkhd->bhqk', q, k_all, preferred_element_type=jnp.float32) * ATTN_SCALE
    s = s + (c_q[..., None] - c[:, :, None, :])
    q_pos = P + jnp.arange(L)
    key_pos = jnp.arange(P + L)
    s = jnp.where(key_pos[None, :] <= q_pos[:, None], s, -jnp.inf)
    p = jax.nn.softmax(s, axis=-1)
    o = jnp.einsum('bhqk,bkhd->bqhd', p.astype(v_all.dtype), v_all)
    return o.reshape(Bsz, L, D_C)


def gather_pages(pool, page_table):
    rows = pool[page_table]
    return rows.reshape((page_table.shape[0], page_table.shape[1] * pool.shape[1]) + pool.shape[2:])


def hier_moe(h, router_group, router_group_bias, router_expert, router_expert_bias, w_gate, w_up, w_down):
    Bsz, L, D = h.shape
    t = h.reshape(Bsz * L, D)
    g_logits = jnp.dot(t, router_group, preferred_element_type=jnp.float32) + router_group_bias.astype(jnp.float32)
    g_prob = jax.nn.softmax(g_logits, axis=-1)
    _, g_sel = lax.top_k(g_logits, 1)
    g_w = jnp.take_along_axis(g_prob, g_sel, axis=-1)
    e_logits = jnp.einsum('td,gde->tge', t, router_expert, preferred_element_type=jnp.float32) + router_expert_bias.astype(jnp.float32)
    e_sel_logits = jnp.take_along_axis(e_logits, g_sel[:, :, None], axis=1)[:, 0]
    e_top, e_idx = lax.top_k(e_sel_logits, TOP_K)
    top_w = jax.nn.softmax(e_top, axis=-1) * g_w
    expert_id = g_sel * E_PER_GROUP + e_idx
    gates = jnp.sum(jax.nn.one_hot(expert_id, N_EXPERTS, dtype=jnp.float32) * top_w[..., None], axis=1)
    out = jnp.zeros((Bsz * L, D), jnp.float32)
    for e in range(N_EXPERTS):
        act = jax.nn.silu(t @ w_gate[e]) * (t @ w_up[e])
        out = out + gates[:, e:e + 1] * (act @ w_down[e]).astype(jnp.float32)
    return out.reshape(Bsz, L, D).astype(h.dtype)


def decoder_layer(x, lru_h0, lru_conv0, ssm_h0, ssm_conv0, k_past, v_past, logf_past, lp):
    Bsz, L, _ = x.shape
    h = rmsnorm(x, lp['norm_mix'])
    u = h @ lp['w_in']
    xa, ga, z, xbc, dt_raw, q, k, v, f_raw = split_proj(u)
    y_a, lru_h, lru_conv = rglru_mixer(xa, ga, lru_h0, lru_conv0, lp['lru_conv_w'], lp['lru_conv_b'],
                                       lp['lru_w_a'], lp['lru_b_a'], lp['lru_w_x'], lp['lru_b_x'], lp['lru_lambda'])
    y_b, ssm_h, ssm_conv = ssd_mixer(z, xbc, dt_raw, ssm_h0, ssm_conv0, lp['ssm_conv_w'], lp['ssm_conv_b'],
                                     lp['ssm_dt_bias'], lp['ssm_a_log'], lp['ssm_d'], lp['ssm_norm'])
    q = rmsnorm(q.reshape(Bsz, L, H_C, HEAD_DIM), lp['attn_q_norm'])
    k = rmsnorm(k.reshape(Bsz, L, H_C, HEAD_DIM), lp['attn_k_norm'])
    v = v.reshape(Bsz, L, H_C, HEAD_DIM)
    logf = jax.nn.log_sigmoid(f_raw.astype(jnp.float32) + lp['attn_f_bias'].astype(jnp.float32))
    if k_past is None:
        y_c = fox_prompt(q, k, v, logf)
    else:
        y_c = fox_sample(q, k, v, logf, k_past, v_past, logf_past)
    x = x + jnp.concatenate([y_a, y_b, y_c], axis=-1) @ lp['w_out']
    x = x + hier_moe(rmsnorm(x, lp['norm_ffn']), lp['router_group'], lp['router_group_bias'],
                     lp['router_expert'], lp['router_expert_bias'], lp['moe_w_gate'], lp['moe_w_up'], lp['moe_w_down'])
    return x, (k, v, logf, lru_h, lru_conv, ssm_h, ssm_conv)


def setup_inputs(seed: int = 0) -> dict:
    key = jax.random.key(seed)
    counter = [0]

    def next_key():
        counter[0] += 1
        return jax.random.fold_in(key, counter[0])

    def normal(shape, scale=1.0):
        return scale * jax.random.normal(next_key(), shape, jnp.float32)

    def uniform(shape, lo, hi):
        return jax.random.uniform(next_key(), shape, jnp.float32, lo, hi)

    n_pages = PAST_LEN // PAGE_SIZE
    n_pool = (5 * DEC_BATCH * n_pages) // 4
    perm = jax.random.permutation(next_key(), n_pool)
    page_table = perm[:DEC_BATCH * n_pages].reshape(DEC_BATCH, n_pages).astype(jnp.int32)
    a0 = uniform((DEPTH, D_A), 0.9, 0.999) ** (1.0 / LRU_C)
    lru_lambda = jnp.log(a0) - jnp.log1p(-a0)
    dt0 = jnp.exp(uniform((DEPTH, H_B), math.log(1e-3), math.log(1e-1)))
    ssm_dt_bias = dt0 + jnp.log(-jnp.expm1(-dt0))
    sd = D_MODEL ** -0.5
    return {
        'x_prompt': normal((BATCH, SEQ, D_MODEL)),
        'x_sample': normal((DEC_BATCH, DEC_SEQ, D_MODEL)),
        'cache_k': normal((DEPTH, n_pool, PAGE_SIZE, H_C, HEAD_DIM)),
        'cache_v': normal((DEPTH, n_pool, PAGE_SIZE, H_C, HEAD_DIM)),
        'cache_logf': jax.nn.log_sigmoid(3.0 + normal((DEPTH, n_pool, PAGE_SIZE, H_C))),
        'state_lru': normal((DEPTH, DEC_BATCH, D_A), 0.5),
        'state_lru_conv': normal((DEPTH, DEC_BATCH, CONV_W - 1, D_A)),
        'state_ssm': normal((DEPTH, DEC_BATCH, H_B, HEAD_DIM, N_B), 0.1),
        'state_ssm_conv': normal((DEPTH, DEC_BATCH, CONV_W - 1, CONV_B)),
        'page_table': page_table,
        'norm_mix': 1.0 + normal((DEPTH, D_MODEL), 0.1),
        'w_in': normal((DEPTH, D_MODEL, N_IN), sd),
        'lru_conv_w': normal((DEPTH, CONV_W, D_A), 0.5),
        'lru_conv_b': normal((DEPTH, D_A), 0.01),
        'lru_w_a': normal((DEPTH, H_A, HEAD_DIM, HEAD_DIM), HEAD_DIM ** -0.5),
        'lru_b_a': normal((DEPTH, D_A), 0.01),
        'lru_w_x': normal((DEPTH, H_A, HEAD_DIM, HEAD_DIM), HEAD_DIM ** -0.5),
        'lru_b_x': normal((DEPTH, D_A), 0.01),
        'lru_lambda': lru_lambda,
        'ssm_conv_w': normal((DEPTH, CONV_W, CONV_B), 0.5),
        'ssm_conv_b': normal((DEPTH, CONV_B), 0.01),
        'ssm_dt_bias': ssm_dt_bias,
        'ssm_a_log': jnp.log(uniform((DEPTH, H_B), 1.0, 16.0)),
        'ssm_d': 1.0 + normal((DEPTH, H_B), 0.1),
        'ssm_norm': 1.0 + normal((DEPTH, D_B), 0.1),
        'attn_q_norm': 1.0 + normal((DEPTH, HEAD_DIM), 0.1),
        'attn_k_norm': 1.0 + normal((DEPTH, HEAD_DIM), 0.1),
        'attn_f_bias': 3.0 + normal((DEPTH, H_C), 0.5),
        'w_out': normal((DEPTH, D_MIX, D_MODEL), D_MIX ** -0.5),
        'norm_ffn': 1.0 + normal((DEPTH, D_MODEL), 0.1),
        'router_group': normal((DEPTH, D_MODEL, N_GROUPS), sd),
        'router_group_bias': normal((DEPTH, N_GROUPS), 0.01),
        'router_expert': normal((DEPTH, N_GROUPS, D_MODEL, E_PER_GROUP), sd),
        'router_expert_bias': normal((DEPTH, N_GROUPS, E_PER_GROUP), 0.01),
        'moe_w_gate': normal((DEPTH, N_EXPERTS, D_MODEL, D_FF), sd),
        'moe_w_up': normal((DEPTH, N_EXPERTS, D_MODEL, D_FF), sd),
        'moe_w_down': normal((DEPTH, N_EXPERTS, D_FF, D_MODEL), D_FF ** -0.5),
    }


def _stacked(states, j):
    return jnp.stack([s[j] for s in states], axis=0)


def reference(x_prompt, x_sample, cache_k, cache_v, cache_logf, state_lru, state_lru_conv, state_ssm,
              state_ssm_conv, page_table, norm_mix, w_in, lru_conv_w, lru_conv_b, lru_w_a, lru_b_a, lru_w_x,
              lru_b_x, lru_lambda, ssm_conv_w, ssm_conv_b, ssm_dt_bias, ssm_a_log, ssm_d, ssm_norm,
              attn_q_norm, attn_k_norm, attn_f_bias, w_out, norm_ffn, router_group, router_group_bias,
              router_expert, router_expert_bias, moe_w_gate, moe_w_up, moe_w_down):
    xp, xs = x_prompt, x_sample
    bp = xp.shape[0]
    new_p, new_s = [], []
    for l in range(DEPTH):
        lp = {
            'norm_mix': norm_mix[l], 'w_in': w_in[l],
            'lru_conv_w': lru_conv_w[l], 'lru_conv_b': lru_conv_b[l], 'lru_w_a': lru_w_a[l],
            'lru_b_a': lru_b_a[l], 'lru_w_x': lru_w_x[l], 'lru_b_x': lru_b_x[l], 'lru_lambda': lru_lambda[l],
            'ssm_conv_w': ssm_conv_w[l], 'ssm_conv_b': ssm_conv_b[l], 'ssm_dt_bias': ssm_dt_bias[l],
            'ssm_a_log': ssm_a_log[l], 'ssm_d': ssm_d[l], 'ssm_norm': ssm_norm[l],
            'attn_q_norm': attn_q_norm[l], 'attn_k_norm': attn_k_norm[l], 'attn_f_bias': attn_f_bias[l],
            'w_out': w_out[l], 'norm_ffn': norm_ffn[l],
            'router_group': router_group[l], 'router_group_bias': router_group_bias[l],
            'router_expert': router_expert[l], 'router_expert_bias': router_expert_bias[l],
            'moe_w_gate': moe_w_gate[l], 'moe_w_up': moe_w_up[l], 'moe_w_down': moe_w_down[l],
        }
        xp, st_p = decoder_layer(
            xp,
            jnp.zeros((bp, D_A), xp.dtype),
            jnp.zeros((bp, CONV_W - 1, D_A), xp.dtype),
            jnp.zeros((bp, H_B, HEAD_DIM, N_B), xp.dtype),
            jnp.zeros((bp, CONV_W - 1, CONV_B), xp.dtype),
            None, None, None, lp)
        xs, st_s = decoder_layer(
            xs, state_lru[l], state_lru_conv[l], state_ssm[l], state_ssm_conv[l],
            gather_pages(cache_k[l], page_table),
            gather_pages(cache_v[l], page_table),
            gather_pages(cache_logf[l], page_table), lp)
        new_p.append(st_p)
        new_s.append(st_s)
    return (xp, xs,
            _stacked(new_p, 0), _stacked(new_p, 1), _stacked(new_p, 2), _stacked(new_p, 3),
            _stacked(new_p, 4), _stacked(new_p, 5), _stacked(new_p, 6),
            _stacked(new_s, 0), _stacked(new_s, 1), _stacked(new_s, 2), _stacked(new_s, 3),
            _stacked(new_s, 4), _stacked(new_s, 5), _stacked(new_s, 6))
```

```python
import functools
import math

import jax
import jax.numpy as jnp
from jax import lax
from jax.experimental import pallas as pl
from jax.experimental.pallas import tpu as pltpu

F32 = jnp.float32
BF16 = jnp.bfloat16

D_MODEL = 2048
HEAD_DIM = 64
D_A = 512
H_A = 8
D_B = 768
H_B = 12
G_B = 4
N_B = 128
CONV_B = D_B + 2 * G_B * N_B
D_C = 768
H_C = 12
CONV_W = 4
LRU_C = 8.0
SSD_CHUNK = 128
PAGE_SIZE = 128
ATTN_SCALE = 1.0 / math.sqrt(HEAD_DIM)
PROJ_SPLITS = (D_A, D_A, D_B, CONV_B, H_B, D_C, D_C, D_C, H_C)
N_GROUPS = 4
E_PER_GROUP = 4
N_EXPERTS = 16
D_FF = 512
EPS = 1e-6
NEG_BIG = -1e30

LANES = 128
SUBLANES = 8
VMEM_LIMIT = 56 * 1024 * 1024

TAIL = 256
ROW_TILE = 640

OFF_Z = 0
OFF_XBC = 768
OFF_Q = 2560
OFF_K = 3328
OFF_V = 4096
OFF_DT = 4864
OFF_F = 4992
OFF_XA = 5120
OFF_GA = 5632
N_U = 6144


def _cparams(*sem):
    return pltpu.CompilerParams(dimension_semantics=sem, vmem_limit_bytes=VMEM_LIMIT)


_NN = (((1,), (0,)), ((), ()))
_NT = (((1,), (1,)), ((), ()))
_TN = (((0,), (0,)), ((), ()))


def _dg(a, b, dims=_NN):
    return lax.dot_general(a, b, dims, preferred_element_type=F32)


def _dot(a, b):
    return _dg(a, b, _NN)


def _split2(x):
    hi = x.astype(BF16)
    lo = (x - hi.astype(F32)).astype(BF16)
    return hi, lo


def _split3(x):
    hi = x.astype(BF16)
    r = x - hi.astype(F32)
    mid = r.astype(BF16)
    lo = (r - mid.astype(F32)).astype(BF16)
    return hi, mid, lo


def _dot_sel_r(x, sel):
    hi, mid, lo = _split3(x)
    return _dot(hi, sel) + _dot(mid, sel) + _dot(lo, sel)


def _dot_sel_l(sel, x):
    hi, mid, lo = _split3(x)
    return _dot(sel, hi) + _dot(sel, mid) + _dot(sel, lo)


def _dg3(a, b, dims=_NN):
    ah, al = _split2(a)
    bh, bl = _split2(b)
    return _dg(ah, bh, dims) + _dg(al, bh, dims) + _dg(ah, bl, dims)


def _mm(a, b, dims, precise):
    if precise:
        return _dg3(a, b, dims)
    return _dg(a.astype(BF16), b.astype(BF16), dims)


def _sigmoid(x):
    return 1.0 / (1.0 + jnp.exp(-x))


def _silu(x):
    return x * _sigmoid(x)


def _log_sigmoid(x):
    return jnp.minimum(x, 0.0) - jnp.log1p(jnp.exp(-jnp.abs(x)))


def _softplus(x):
    return jnp.maximum(x, 0.0) + jnp.log1p(jnp.exp(-jnp.abs(x)))


def _gelu_tanh(x):
    return 0.5 * x * (1.0 + jnp.tanh(math.sqrt(2.0 / math.pi) * (x + 0.044715 * (x * x * x))))


def _iota(shape, dim):
    return lax.broadcasted_iota(jnp.int32, shape, dim)


def _seg_matrix(n_rows, n_cols, seg, transpose=False):
    if transpose:
        m = (_iota((n_cols, n_rows), 1) // seg) == _iota((n_cols, n_rows), 0)
    else:
        m = (_iota((n_rows, n_cols), 0) // seg) == _iota((n_rows, n_cols), 1)
    return jnp.where(m, 1.0, 0.0).astype(BF16)


def _row_block(b, i, r, nb, seq):
    n_bulk = (seq - TAIL) // r
    n_tail = TAIL // r
    return jnp.where(i < n_bulk, b * n_bulk + i, nb * n_bulk + b * n_tail + (i - n_bulk))


def _const_spec(shape, n_grid):
    zeros = (0,) * len(shape)
    return pl.BlockSpec(shape, lambda *_: zeros, pipeline_mode=pl.Buffered(1))


def _in_proj_body(x_ref, g_ref, wh_ref, wl_ref, o_ref, xh_ref, xl_ref):
    last = pl.program_id(0) == pl.num_programs(0) - 1

    @pl.when(pl.program_id(1) == 0)
    def _():
        x = x_ref[...]
        ms = jnp.mean(x * x, axis=-1, keepdims=True)
        xn = x * lax.rsqrt(ms + EPS) * g_ref[...]
        hi = xn.astype(BF16)
        xh_ref[...] = hi
        xl_ref[...] = (xn - hi.astype(F32)).astype(BF16)

    @pl.when(jnp.logical_not(last))
    def _():
        o_ref[...] = _dot(xh_ref[...], wh_ref[...])

    @pl.when(last)
    def _():
        o_ref[...] = (_dot(xh_ref[...], wh_ref[...]) + _dot(xl_ref[...], wh_ref[...])
                      + _dot(xh_ref[...], wl_ref[...]))


def _in_proj(x, g, wh, wl, *, tn=1024):
    t = x.shape[0]
    tm = ROW_TILE
    n_i = t // tm
    return pl.pallas_call(
        _in_proj_body,
        grid=(n_i, N_U // tn),
        in_specs=[
            pl.BlockSpec((tm, D_MODEL), lambda i, j: (i, 0)),
            pl.BlockSpec((1, D_MODEL), lambda i, j: (0, 0)),
            pl.BlockSpec((D_MODEL, tn), lambda i, j: (0, j)),
            pl.BlockSpec((D_MODEL, tn), lambda i, j: (0, jnp.where(i == n_i - 1, j, 0))),
        ],
        out_specs=pl.BlockSpec((tm, tn), lambda i, j: (i, j)),
        out_shape=jax.ShapeDtypeStruct((t, N_U), F32),
        scratch_shapes=[pltpu.VMEM((tm, D_MODEL), BF16), pltpu.VMEM((tm, D_MODEL), BF16)],
        compiler_params=_cparams("parallel", "arbitrary"),
        name="in_proj",
    )(x, g, wh, wl)


def _lru_gates(xc, wa, ba, wx, bx, lam):
    r = _sigmoid(_dg3(xc, wa) + ba)
    i = _sigmoid(_dg3(xc, wx) + bx)
    log_a = LRU_C * r * _log_sigmoid(lam)
    a = jnp.exp(log_a)
    th = jnp.tanh(log_a)
    mult = jnp.sqrt(-2.0 * th / (1.0 - th))
    return a, mult * (i * xc)


def _lru_prompt_body(u_ref, cw_ref, cb_ref, wa_ref, ba_ref, wx_ref, bx_ref, lam_ref,
                     y_ref, h_out_ref, conv_out_ref, xp_ref, h_ref, *, tl):
    i = pl.program_id(1)

    @pl.when(i == 0)
    def _():
        xp_ref[0:SUBLANES, :] = jnp.zeros((SUBLANES, D_A), F32)
        h_ref[...] = jnp.zeros((1, D_A), F32)

    x = u_ref[:, 0:D_A]
    ga = u_ref[:, D_A:2 * D_A]
    xp_ref[SUBLANES:SUBLANES + tl, :] = x
    cw = cw_ref[...]
    xc = cb_ref[...] + cw[3:4, :] * x
    for j in range(CONV_W - 1):
        xc = xc + cw[j:j + 1, :] * xp_ref[pl.ds(SUBLANES - 3 + j, tl), :]
    xp_ref[0:SUBLANES, :] = x[tl - SUBLANES:tl, :]

    a, u = _lru_gates(xc, wa_ref[...], ba_ref[...], wx_ref[...], bx_ref[...], lam_ref[...])

    row = _iota((tl, D_A), 0)
    s = 1
    while s < tl:
        keep = row >= s
        a_sh = jnp.where(keep, pltpu.roll(a, s, axis=0), 1.0)
        u_sh = jnp.where(keep, pltpu.roll(u, s, axis=0), 0.0)
        u = u + a * u_sh
        a = a * a_sh
        s *= 2
    h = a * h_ref[...] + u
    h_ref[...] = h[tl - 1:tl, :]
    y_ref[...] = h * _gelu_tanh(ga)

    @pl.when(i == pl.num_programs(1) - 1)
    def _():
        h_out_ref[0] = h[tl - 1:tl, :]
        conv_out_ref[0] = x[tl - 3:tl, :]


def _lru_prompt(u, nb, seq, cw, cb, wa, ba, wx, bx, lam, *, tl=256):
    nt = seq // tl
    vec = lambda n: pl.BlockSpec((1, n), lambda b, i: (0, 0))
    mat = lambda r, c: pl.BlockSpec((r, c), lambda b, i: (0, 0))
    rb = lambda b, i: _row_block(b, i, tl, nb, seq)
    return pl.pallas_call(
        functools.partial(_lru_prompt_body, tl=tl),
        grid=(nb, nt),
        in_specs=[
            pl.BlockSpec((tl, 2 * D_A), lambda b, i: (rb(b, i), OFF_XA // (2 * D_A))),
            mat(CONV_W, D_A), vec(D_A), mat(D_A, D_A), vec(D_A), mat(D_A, D_A), vec(D_A), vec(D_A),
        ],
        out_specs=[
            pl.BlockSpec((tl, D_A), lambda b, i: (rb(b, i), 0)),
            pl.BlockSpec((1, 1, D_A), lambda b, i: (b, 0, 0)),
            pl.BlockSpec((1, CONV_W - 1, D_A), lambda b, i: (b, 0, 0)),
        ],
        out_shape=[
            jax.ShapeDtypeStruct((nb * seq, D_A), F32),
            jax.ShapeDtypeStruct((nb, 1, D_A), F32),
            jax.ShapeDtypeStruct((nb, CONV_W - 1, D_A), F32),
        ],
        scratch_shapes=[pltpu.VMEM((SUBLANES + tl, D_A), F32), pltpu.VMEM((1, D_A), F32)],
        compiler_params=_cparams("parallel", "arbitrary"),
        name="lru_prompt",
    )(u, cw, cb, wa, ba, wx, bx, lam)


def _group_rmsnorm(y, g):
    gw = D_B // G_B
    ss = _dot_sel_r(y * y, _seg_matrix(D_B, LANES, gw))
    rs = lax.rsqrt(ss * (1.0 / gw) + EPS)
    return y * _dot_sel_r(rs, _seg_matrix(D_B, LANES, gw, transpose=True)) * g


def _ssd_chunk(x, bm, cm, dt, dta, dsk, h_ref, precise):
    q = SSD_CHUNK
    tri = jnp.where(_iota((q, q), 0) >= _iota((q, q), 1), 1.0, 0.0).astype(BF16)
    cum = _dot_sel_l(tri, dta)
    cum_t = cum.T
    dt_t = dt.T
    causal = _iota((q, q), 0) >= _iota((q, q), 1)
    ys = []
    for g in range(G_B):
        bg = bm[:, g * N_B:(g + 1) * N_B]
        cg = cm[:, g * N_B:(g + 1) * N_B]
        scores = _mm(cg, bg, _NT, precise)
        for r in range(H_B // G_B):
            h = g * (H_B // G_B) + r
            xh = x[:, h * HEAD_DIM:(h + 1) * HEAD_DIM]
            cl = cum[:, h:h + 1]
            cs = cum_t[h:h + 1, :]
            c_last = cum_t[h:h + 1, q - 1:q]
            decay = jnp.exp(jnp.where(causal, cl - cs, NEG_BIG))
            w = scores * decay * dt_t[h:h + 1, :]
            y_diag = _mm(w, xh, _NN, precise)
            h_prev = h_ref[h]
            y_off = _mm(cg, h_prev, _NT, precise) * jnp.exp(cl)
            de = jnp.exp(c_last - cl) * dt[:, h:h + 1]
            st = _mm(xh * de, bg, _TN, precise)
            h_ref[h] = jnp.exp(c_last) * h_prev + st
            ys.append(y_diag + y_off + dsk[:, h:h + 1] * xh)
    return jnp.concatenate(ys, axis=1)


def _ssd_prompt_body(u_ref, s_ref, cw_ref, cb_ref, dtb_ref, alog_ref, dsk_ref, ng_ref,
                     y_ref, h_out_ref, conv_out_ref, xp_ref, h_ref, *, precise_tail):
    c = pl.program_id(1)
    nc = pl.num_programs(1)
    q = SSD_CHUNK

    @pl.when(c == 0)
    def _():
        xp_ref[0:SUBLANES, :] = jnp.zeros((SUBLANES, CONV_B), F32)
        h_ref[...] = jnp.zeros((H_B, HEAD_DIM, N_B), F32)

    z = u_ref[:, OFF_Z:OFF_Z + D_B]
    xbc = u_ref[:, OFF_XBC:OFF_XBC + CONV_B]
    xp_ref[SUBLANES:SUBLANES + q, :] = xbc
    cw = cw_ref[...]
    xc = cb_ref[...] + cw[3:4, :] * xbc
    for j in range(CONV_W - 1):
        xc = xc + cw[j:j + 1, :] * xp_ref[pl.ds(SUBLANES - 3 + j, q), :]
    xp_ref[0:SUBLANES, :] = xbc[q - SUBLANES:q, :]
    xc = _silu(xc)
    x = xc[:, 0:D_B]
    bm = xc[:, D_B:D_B + G_B * N_B]
    cm = xc[:, D_B + G_B * N_B:]
    dt = _softplus(s_ref[:, 0:LANES] + dtb_ref[...])
    dta = dt * (-jnp.exp(alog_ref[...]))
    gate = _silu(z)

    def finish(precise):
        y = _ssd_chunk(x, bm, cm, dt, dta, dsk_ref[...], h_ref, precise) * gate
        y_ref[...] = _group_rmsnorm(y, ng_ref[...])

    if precise_tail:
        in_tail = c >= nc - TAIL // q
        pl.when(in_tail)(lambda: finish(True))
        pl.when(jnp.logical_not(in_tail))(lambda: finish(False))
    else:
        finish(False)

    @pl.when(c == nc - 1)
    def _():
        h_out_ref[0] = h_ref[...]
        conv_out_ref[0] = xbc[q - 3:q, :]


def _ssd_prompt(u, nb, seq, cw, cb, dtb, alog, dsk, ng, *, precise_tail):
    nc = seq // SSD_CHUNK
    q = SSD_CHUNK
    vec = lambda n: pl.BlockSpec((1, n), lambda b, c: (0, 0))
    rb = lambda b, c: _row_block(b, c, q, nb, seq)
    return pl.pallas_call(
        functools.partial(_ssd_prompt_body, precise_tail=precise_tail),
        grid=(nb, nc),
        in_specs=[
            pl.BlockSpec((q, OFF_Q), lambda b, c: (rb(b, c), 0)),
            pl.BlockSpec((q, 2 * LANES), lambda b, c: (rb(b, c), OFF_DT // (2 * LANES))),
            pl.BlockSpec((CONV_W, CONV_B), lambda b, c: (0, 0)),
            vec(CONV_B), vec(LANES), vec(LANES), vec(LANES), vec(D_B),
        ],
        out_specs=[
            pl.BlockSpec((q, D_B), lambda b, c: (rb(b, c), 0)),
            pl.BlockSpec((1, H_B, HEAD_DIM, N_B), lambda b, c: (b, 0, 0, 0)),
            pl.BlockSpec((1, CONV_W - 1, CONV_B), lambda b, c: (b, 0, 0)),
        ],
        out_shape=[
            jax.ShapeDtypeStruct((nb * seq, D_B), F32),
            jax.ShapeDtypeStruct((nb, H_B, HEAD_DIM, N_B), F32),
            jax.ShapeDtypeStruct((nb, CONV_W - 1, CONV_B), F32),
        ],
        scratch_shapes=[pltpu.VMEM((SUBLANES + q, CONV_B), F32),
                        pltpu.VMEM((H_B, HEAD_DIM, N_B), F32)],
        compiler_params=_cparams("parallel", "arbitrary"),
        name="ssd_prompt",
    )(u, u, cw, cb, dtb, alog, dsk, ng)


def _head_rmsnorm(x, g):
    ss = _dot_sel_r(x * x, _seg_matrix(D_C, LANES, HEAD_DIM))
    rs = lax.rsqrt(ss * (1.0 / HEAD_DIM) + EPS)
    return x * _dot_sel_r(rs, _seg_matrix(D_C, LANES, HEAD_DIM, transpose=True)) * g


def _qk_prep_prompt_body(u_ref, gq_ref, gk_ref, fb_ref, qh_ref, ql_ref, k_ref, kh_ref, kl_ref,
                         v_ref, vh_ref, vl_ref, lf_ref, ct_ref, carry_ref, *, tl):
    q = u_ref[:, 0:D_C]
    k = u_ref[:, D_C:2 * D_C]
    v = u_ref[:, 2 * D_C:3 * D_C]
    f_raw = u_ref[:, OFF_F - OFF_Q:OFF_F - OFF_Q + LANES]
    qh_ref[...], ql_ref[...] = _split2(_head_rmsnorm(q, gq_ref[...]) * ATTN_SCALE)
    kn = _head_rmsnorm(k, gk_ref[...])
    k_ref[...] = kn
    kh_ref[...], kl_ref[...] = _split2(kn)
    v_ref[...] = v
    vh_ref[...], vl_ref[...] = _split2(v)
    lf = _log_sigmoid(f_raw + fb_ref[...])
    lf_ref[...] = lf

    @pl.when(pl.program_id(1) == 0)
    def _():
        carry_ref[...] = jnp.zeros((1, LANES), F32)

    tri = jnp.where(_iota((tl, tl), 0) >= _iota((tl, tl), 1), 1.0, 0.0).astype(BF16)
    c = _dot_sel_l(tri, lf) + carry_ref[...]
    carry_ref[...] = c[tl - 1:tl, :]
    ct_ref[0] = c.T[0:2 * SUBLANES, :]


def _qk_prep_prompt(u, nb, seq, gq, gk, fb, *, tl=256):
    nt = seq // tl
    rows = nb * seq
    wq = OFF_XA - OFF_Q
    vec = lambda n: pl.BlockSpec((1, n), lambda b, i: (0, 0))
    nat = lambda w: pl.BlockSpec((tl, w), lambda b, i: (b * nt + i, 0))
    s16 = jax.ShapeDtypeStruct((rows, D_C), BF16)
    s32 = jax.ShapeDtypeStruct((rows, D_C), F32)
    return pl.pallas_call(
        functools.partial(_qk_prep_prompt_body, tl=tl),
        grid=(nb, nt),
        in_specs=[
            pl.BlockSpec((tl, wq), lambda b, i: (_row_block(b, i, tl, nb, seq), OFF_Q // wq)),
            vec(D_C), vec(D_C), vec(LANES),
        ],
        out_specs=[nat(D_C)] * 8 + [nat(LANES), pl.BlockSpec((1, 2 * SUBLANES, tl), lambda b, i: (b, 0, i))],
        out_shape=[s16, s16, s32, s16, s16, s32, s16, s16,
                   jax.ShapeDtypeStruct((rows, LANES), F32),
                   jax.ShapeDtypeStruct((nb, 2 * SUBLANES, seq), F32)],
        scratch_shapes=[pltpu.VMEM((1, LANES), F32)],
        compiler_params=_cparams("parallel", "arbitrary"),
        name="qk_prep_prompt",
    )(u, gq, gk, fb)


def _qk_prep_sample_body(u_ref, gq_ref, gk_ref, fb_ref, q_ref, k_ref, lf_ref):
    q = u_ref[:, 0:D_C]
    k = u_ref[:, D_C:2 * D_C]
    f_raw = u_ref[:, OFF_F - OFF_Q:OFF_F - OFF_Q + LANES]
    q_ref[...] = _head_rmsnorm(q, gq_ref[...]) * ATTN_SCALE
    k_ref[...] = _head_rmsnorm(k, gk_ref[...])
    lf_ref[...] = _log_sigmoid(f_raw + fb_ref[...])


def _qk_prep_sample(u, row0, ns, gq, gk, fb):
    wq = OFF_XA - OFF_Q
    full = lambda r, c: pl.BlockSpec((r, c), lambda i: (0, 0))
    return pl.pallas_call(
        _qk_prep_sample_body,
        grid=(1,),
        in_specs=[pl.BlockSpec((ns, wq), lambda i: (row0 // ns, OFF_Q // wq)),
                  full(1, D_C), full(1, D_C), full(1, LANES)],
        out_specs=[full(ns, D_C), full(ns, D_C), full(ns, LANES)],
        out_shape=[jax.ShapeDtypeStruct((ns, D_C), F32), jax.ShapeDtypeStruct((ns, D_C), F32),
                   jax.ShapeDtypeStruct((ns, LANES), F32)],
        compiler_params=_cparams("arbitrary"),
        name="qk_prep_sample",
    )(u, gq, gk, fb)


def _fox_tile(i, refs, tq, precise):
    qh_ref, ql_ref, kh_ref, kl_ref, vh_ref, vl_ref, ct_ref, o_ref = refs
    lane = _iota((tq, LANES), 1)
    causal = _iota((tq, tq), 0) >= _iota((tq, tq), 1)
    outs = []
    for hh in range(2):
        in_head = (lane >= hh * HEAD_DIM) & (lane < (hh + 1) * HEAD_DIM)
        zero = jnp.zeros((tq, LANES), BF16)
        qh = jnp.where(in_head, qh_ref[...], zero)
        ql = jnp.where(in_head, ql_ref[...], zero) if precise else None
        cq0 = ct_ref[0, 0, hh:hh + 1, pl.ds(pl.multiple_of(i * tq, tq), LANES)][:, 0:1]

        def step(j, carry, masked, qh=qh, ql=ql, cq0=cq0, hh=hh):
            m, l, acc = carry
            off = pl.multiple_of(j * tq, tq)
            kh = kh_ref[pl.ds(off, tq), :]
            vh = vh_ref[pl.ds(off, tq), :]
            s = _dg(qh, kh, _NT)
            if precise:
                kl = kl_ref[pl.ds(off, tq), :]
                s = s + _dg(ql, kh, _NT) + _dg(qh, kl, _NT)
            s = s + (cq0 - ct_ref[0, 0, hh:hh + 1, pl.ds(off, tq)])
            if masked:
                s = jnp.where(causal, s, NEG_BIG)
            m_new = jnp.maximum(m, jnp.max(s, axis=1, keepdims=True))
            alpha = jnp.exp(m - m_new)
            p = jnp.exp(s - m_new)
            l = alpha * l + jnp.sum(p, axis=1, keepdims=True)
            if precise:
                ph, pl_ = _split2(p)
                vl = vl_ref[pl.ds(off, tq), :]
                pv = _dot(ph, vh) + _dot(pl_, vh) + _dot(ph, vl)
            else:
                pv = _dot(p.astype(BF16), vh)
            return m_new, l, alpha * acc + pv

        init = (jnp.full((tq, 1), NEG_BIG, F32), jnp.zeros((tq, 1), F32), jnp.zeros((tq, LANES), F32))
        carry = lax.fori_loop(0, i, functools.partial(step, masked=False), init)
        m, l, acc = step(i, carry, True)
        outs.append(acc / l)
    o_ref[...] = jnp.where(lane < HEAD_DIM, outs[0], outs[1])


def _fox_body(*refs, tq, precise_tail):
    i = pl.program_id(2)
    if precise_tail:
        last = i == pl.num_programs(2) - 1
        pl.when(last)(lambda: _fox_tile(i, refs, tq, True))
        pl.when(jnp.logical_not(last))(lambda: _fox_tile(i, refs, tq, False))
    else:
        _fox_tile(i, refs, tq, False)


def _fox_prompt(qh, ql, kh, kl, vh, vl, ct, nb, seq, *, precise_tail, tq=TAIL):
    nq = seq // tq
    npair = H_C // 2
    qspec = pl.BlockSpec((tq, LANES), lambda b, p, i: (b * nq + i, p))
    kspec = pl.BlockSpec((seq, LANES), lambda b, p, i: (b, p))
    return pl.pallas_call(
        functools.partial(_fox_body, tq=tq, precise_tail=precise_tail),
        grid=(nb, npair, nq),
        in_specs=[qspec, qspec, kspec, kspec, kspec, kspec,
                  pl.BlockSpec((1, 1, SUBLANES, seq), lambda b, p, i: (b, p, 0, 0))],
        out_specs=pl.BlockSpec((tq, LANES), lambda b, p, i: (_row_block(b, i, tq, nb, seq), p)),
        out_shape=jax.ShapeDtypeStruct((nb * seq, D_C), F32),
        compiler_params=_cparams("parallel", "parallel", "arbitrary"),
        name="fox_prompt",
    )(qh, ql, kh, kl, vh, vl, ct)


def _sample_pre_body(u_ref, hl_ref, cl_ref, cs_ref, lcw_ref, lcb_ref, wa_ref, ba_ref, wx_ref, bx_ref,
                     lam_ref, scw_ref, scb_ref, dtb_ref, alog_ref, dsk_ref,
                     ya_ref, hnew_ref, lconv_ref, sconv_ref, dtxt_ref, dat_ref, bm_ref, cm_ref,
                     xd_ref, zg_ref):
    xa = u_ref[:, OFF_XA:OFF_XA + D_A]
    ga = u_ref[:, OFF_GA:OFF_GA + D_A]
    cw = lcw_ref[...]
    xc = lcb_ref[...] + cw[3:4, :] * xa
    for j in range(CONV_W - 1):
        xc = xc + cw[j:j + 1, :] * cl_ref[j]
    lconv_ref[0] = cl_ref[1]
    lconv_ref[1] = cl_ref[2]
    lconv_ref[2] = xa
    a, uu = _lru_gates(xc, wa_ref[...], ba_ref[...], wx_ref[...], bx_ref[...], lam_ref[...])
    h = a * hl_ref[...] + uu
    hnew_ref[...] = h
    ya_ref[...] = h * _gelu_tanh(ga)

    z = u_ref[:, OFF_Z:OFF_Z + D_B]
    xbc = u_ref[:, OFF_XBC:OFF_XBC + CONV_B]
    cw = scw_ref[...]
    xs = scb_ref[...] + cw[3:4, :] * xbc
    for j in range(CONV_W - 1):
        xs = xs + cw[j:j + 1, :] * cs_ref[j]
    sconv_ref[0] = cs_ref[1]
    sconv_ref[1] = cs_ref[2]
    sconv_ref[2] = xbc
    xs = _silu(xs)
    x = xs[:, 0:D_B]
    bm_ref[...] = xs[:, D_B:D_B + G_B * N_B]
    cm_ref[...] = xs[:, D_B + G_B * N_B:]
    dt = _softplus(u_ref[:, OFF_DT:OFF_DT + LANES] + dtb_ref[...])
    da = jnp.exp(dt * (-jnp.exp(alog_ref[...])))
    expand = _seg_matrix(D_B, LANES, HEAD_DIM, transpose=True)
    dtxt_ref[...] = (_dot_sel_r(dt, expand) * x).T
    dat_ref[...] = _dot_sel_r(da, expand).T
    xd_ref[...] = dsk_ref[...] * x
    zg_ref[...] = _silu(z)


def _sample_pre(u, row0, ns, h_lru, conv_lru, conv_ssm, lcw, lcb, wa, ba, wx, bx, lam, scw, scb, dtb, alog, dsk_e):
    full = lambda *shape: pl.BlockSpec(shape, lambda i: (0,) * len(shape))
    f = lambda *shape: jax.ShapeDtypeStruct(shape, F32)
    return pl.pallas_call(
        _sample_pre_body,
        grid=(1,),
        in_specs=[
            pl.BlockSpec((ns, N_U), lambda i: (row0 // ns, 0)),
            full(ns, D_A), full(CONV_W - 1, ns, D_A), full(CONV_W - 1, ns, CONV_B),
            full(CONV_W, D_A), full(1, D_A), full(D_A, D_A), full(1, D_A), full(D_A, D_A), full(1, D_A),
            full(1, D_A), full(CONV_W, CONV_B), full(1, CONV_B), full(1, LANES), full(1, LANES), full(1, D_B),
        ],
        out_specs=[
            full(ns, D_A), full(ns, D_A), full(CONV_W - 1, ns, D_A), full(CONV_W - 1, ns, CONV_B),
            full(D_B, ns), full(D_B, ns), full(ns, G_B * N_B), full(ns, G_B * N_B), full(ns, D_B), full(ns, D_B),
        ],
        out_shape=[
            f(ns, D_A), f(ns, D_A), f(CONV_W - 1, ns, D_A), f(CONV_W - 1, ns, CONV_B),
            f(D_B, ns), f(D_B, ns), f(ns, G_B * N_B), f(ns, G_B * N_B), f(ns, D_B), f(ns, D_B),
        ],
        compiler_params=_cparams("arbitrary"),
        name="sample_pre",
    )(u, h_lru, conv_lru, conv_ssm, lcw, lcb, wa, ba, wx, bx, lam, scw, scb, dtb, alog, dsk_e)


def _lane_bcast_column(parts, b, ns):
    onehot = jnp.where(_iota((ns, LANES), 0) == b, 1.0, 0.0).astype(BF16)
    out = _dot(parts[0], onehot)
    for p in parts[1:]:
        out = out + _dot(p, onehot)
    return out


def _ssd_update_body(st_ref, dtxt_ref, dat_ref, bm_ref, cm_ref, xd_ref, zg_ref, ng_ref,
                     st_out_ref, y_ref, yt_ref, *, tb, ns):
    i = pl.program_id(0)

    @pl.when(i == 0)
    def _():
        yt_ref[...] = jnp.zeros((D_B, ns), F32)

    dtx3 = _split3(dtxt_ref[...])
    da3 = _split3(dat_ref[...])
    lane = _iota((D_B, ns), 1)
    rows_per_group = D_B // G_B

    def group_rows(row):
        return jnp.concatenate(
            [jnp.broadcast_to(row[:, g * N_B:(g + 1) * N_B], (rows_per_group, N_B)) for g in range(G_B)], axis=0)

    for t in range(tb):
        b = i * tb + t
        xb = _lane_bcast_column(dtx3, b, ns)
        dab = _lane_bcast_column(da3, b, ns)
        bexp = group_rows(bm_ref[pl.ds(b, 1), :])
        cexp = group_rows(cm_ref[pl.ds(b, 1), :])
        s_new = dab * st_ref[t].reshape(D_B, N_B) + xb * bexp
        st_out_ref[t] = s_new.reshape(H_B, HEAD_DIM, N_B)
        ycol = jnp.sum(s_new * cexp, axis=1, keepdims=True)
        yt_ref[...] = jnp.where(lane == b, ycol, yt_ref[...])

    @pl.when(i == pl.num_programs(0) - 1)
    def _():
        y = (yt_ref[...].T + xd_ref[...]) * zg_ref[...]
        y_ref[...] = _group_rmsnorm(y, ng_ref[...])


def _ssd_update(state, layer, dtxt, dat, bm, cm, xd, zg, ng, *, tb=8):
    ns = state.shape[1]
    full = lambda *shape: pl.BlockSpec(shape, lambda i: (0,) * len(shape))
    return pl.pallas_call(
        functools.partial(_ssd_update_body, tb=tb, ns=ns),
        grid=(ns // tb,),
        in_specs=[
            pl.BlockSpec((None, tb, H_B, HEAD_DIM, N_B), lambda i: (layer, i, 0, 0, 0)),
            full(D_B, ns), full(D_B, ns), full(ns, G_B * N_B), full(ns, G_B * N_B),
            full(ns, D_B), full(ns, D_B), full(1, D_B),
        ],
        out_specs=[
            pl.BlockSpec((tb, H_B, HEAD_DIM, N_B), lambda i: (i, 0, 0, 0)),
            full(ns, D_B),
        ],
        out_shape=[
            jax.ShapeDtypeStruct((ns, H_B, HEAD_DIM, N_B), F32),
            jax.ShapeDtypeStruct((ns, D_B), F32),
        ],
        scratch_shapes=[pltpu.VMEM((D_B, ns), F32)],
        compiler_params=_cparams("arbitrary"),
        name="ssd_update",
    )(state, dtxt, dat, bm, cm, xd, zg, ng)


QKV_ROWS = 3 * D_C + 2 * SUBLANES


def _paged_body(pt_ref, qkv_ref, *refs, pp, ns):
    k_refs = refs[0:pp]
    v_refs = refs[pp:2 * pp]
    lf_refs = refs[2 * pp:3 * pp]
    ot_ref, qb_ref, m_ref, l_ref, acc_ref, carry_ref = refs[3 * pp:]
    b = pl.program_id(0)
    g = pl.program_id(1)
    lane3 = _iota((H_C, HEAD_DIM, PAGE_SIZE), 2)

    @pl.when(g == 0)
    def _():
        big = _lane_bcast_column(_split3(qkv_ref[...]), b, ns)
        qb = big[0:D_C].reshape(H_C, HEAD_DIM, PAGE_SIZE)
        kb = big[D_C:2 * D_C].reshape(H_C, HEAD_DIM, PAGE_SIZE)
        vb = big[2 * D_C:3 * D_C].reshape(H_C, HEAD_DIM, PAGE_SIZE)
        qb_ref[...] = qb
        m_ref[...] = jnp.sum(qb * kb, axis=1)
        l_ref[...] = jnp.ones((H_C, PAGE_SIZE), F32)
        acc_ref[...] = jnp.where(lane3 == 0, vb, 0.0)
        carry_ref[...] = big[3 * D_C:3 * D_C + H_C]

    @pl.when((b == 0) & (g == 0))
    def _():
        ot_ref[...] = jnp.zeros((D_C, ns), F32)

    later = jnp.where(_iota((PAGE_SIZE, PAGE_SIZE), 0) > _iota((PAGE_SIZE, PAGE_SIZE), 1), 1.0, 0.0).astype(BF16)
    ones = jnp.ones((PAGE_SIZE, PAGE_SIZE), BF16)
    qb = qb_ref[...]
    m = m_ref[...]
    l = l_ref[...]
    acc = acc_ref[...]
    carry = carry_ref[...]
    for t in reversed(range(pp)):
        kt = k_refs[t][0, 0]
        vt = v_refs[t][0, 0]
        lf = lf_refs[t][0, :, 0, 0, :]
        s = jnp.sum(qb * kt, axis=1) + carry + _dot_sel_r(lf, later)
        carry = carry + _dot_sel_r(lf, ones)
        m_new = jnp.maximum(m, jnp.max(s, axis=1, keepdims=True))
        alpha = jnp.exp(m - m_new)
        p = jnp.exp(s - m_new)
        l = alpha * l + jnp.sum(p, axis=1, keepdims=True)
        acc = alpha[:, None, :] * acc + p[:, None, :] * vt
        m = m_new
    m_ref[...] = m
    l_ref[...] = l
    acc_ref[...] = acc
    carry_ref[...] = carry

    @pl.when(g == pl.num_programs(1) - 1)
    def _():
        o = jnp.sum(acc, axis=2, keepdims=True) / l[:, None, :]
        o = o.reshape(D_C, PAGE_SIZE)
        ot_ref[...] = jnp.where(_iota((D_C, ns), 1) == b, o, ot_ref[...])


def _paged_attn(pt_flat, qkv_t, kt_pool, vt_pool, lf_pool, layer, ns, n_pages, *, pp=4):
    ng = n_pages // pp

    def page_map(t):
        return lambda b, g, pt: (layer, pt[b * n_pages + (ng - 1 - g) * pp + t], 0, 0, 0)

    def lf_map(t):
        return lambda b, g, pt: (layer, 0, pt[b * n_pages + (ng - 1 - g) * pp + t], 0, 0)

    kv_specs = [pl.BlockSpec((1, 1, H_C, HEAD_DIM, PAGE_SIZE), page_map(t)) for t in range(pp)]
    lf_specs = [pl.BlockSpec((1, H_C, 1, 1, PAGE_SIZE), lf_map(t)) for t in range(pp)]
    grid_spec = pltpu.PrefetchScalarGridSpec(
        num_scalar_prefetch=1,
        grid=(ns, ng),
        in_specs=[pl.BlockSpec((QKV_ROWS, ns), lambda b, g, pt: (0, 0))] + kv_specs + kv_specs + lf_specs,
        out_specs=pl.BlockSpec((D_C, ns), lambda b, g, pt: (0, 0)),
        scratch_shapes=[
            pltpu.VMEM((H_C, HEAD_DIM, PAGE_SIZE), F32),
            pltpu.VMEM((H_C, PAGE_SIZE), F32),
            pltpu.VMEM((H_C, PAGE_SIZE), F32),
            pltpu.VMEM((H_C, HEAD_DIM, PAGE_SIZE), F32),
            pltpu.VMEM((H_C, PAGE_SIZE), F32),
        ],
    )
    return pl.pallas_call(
        functools.partial(_paged_body, pp=pp, ns=ns),
        grid_spec=grid_spec,
        out_shape=jax.ShapeDtypeStruct((D_C, ns), F32),
        compiler_params=_cparams("arbitrary", "arbitrary"),
        name="paged_attn",
    )(pt_flat, qkv_t, *([kt_pool] * pp), *([vt_pool] * pp), *([lf_pool] * pp))


def _route(lg):
    lane = _iota(lg.shape, 1)
    big = jnp.int32(1 << 20)
    gl = jnp.where(lane < N_GROUPS, lg, NEG_BIG)
    gmax = jnp.max(gl, axis=1, keepdims=True)
    g_sel = jnp.min(jnp.where(gl == gmax, lane, big), axis=1, keepdims=True)
    g_w = 1.0 / jnp.sum(jnp.where(lane < N_GROUPS, jnp.exp(gl - gmax), 0.0), axis=1, keepdims=True)
    in_group = (lane >= N_GROUPS) & (lane < N_GROUPS + N_EXPERTS) & \
        (jnp.right_shift(lane - N_GROUPS, 2) == g_sel)
    el = jnp.where(in_group, lg, NEG_BIG)
    e1 = jnp.max(el, axis=1, keepdims=True)
    i1 = jnp.min(jnp.where(el == e1, lane, big), axis=1, keepdims=True)
    el2 = jnp.where(lane == i1, NEG_BIG, el)
    e2 = jnp.max(el2, axis=1, keepdims=True)
    i2 = jnp.min(jnp.where(el2 == e2, lane, big), axis=1, keepdims=True)
    r = jnp.exp(e2 - e1)
    w1 = g_w / (1.0 + r)
    w2 = g_w * r / (1.0 + r)
    out = jnp.where(lane == 0, (i1 - N_GROUPS).astype(F32), 0.0)
    out = jnp.where(lane == 1, (i2 - N_GROUPS).astype(F32), out)
    out = jnp.where(lane == 2, w1, out)
    out = jnp.where(lane == 3, w2, out)
    return out


def _out_proj_body(x_ref, ya_ref, yb_ref, yc_ref, wah_ref, wbh_ref, wch_ref, wal_ref, wbl_ref, wcl_ref,
                   g_ref, wr_ref, rb_ref, x1_ref, h2_ref, route_ref, *, n_precise):
    i = pl.program_id(0)
    precise = i >= pl.num_programs(0) - n_precise
    ys = (ya_ref, yb_ref, yc_ref)
    whs = (wah_ref, wbh_ref, wch_ref)
    wls = (wal_ref, wbl_ref, wcl_ref)

    @pl.when(jnp.logical_not(precise))
    def _():
        d = _dot(ys[0][...].astype(BF16), whs[0][...])
        for y, wh in zip(ys[1:], whs[1:]):
            d = d + _dot(y[...].astype(BF16), wh[...])
        x1_ref[...] = x_ref[...] + d

    @pl.when(precise)
    def _():
        d = jnp.zeros(x1_ref.shape, F32)
        for y, wh, wl in zip(ys, whs, wls):
            yh, yl = _split2(y[...])
            d = d + _dot(yh, wh[...]) + _dot(yl, wh[...]) + _dot(yh, wl[...])
        x1_ref[...] = x_ref[...] + d

    x1 = x1_ref[...]
    ms = jnp.mean(x1 * x1, axis=-1, keepdims=True)
    h2 = x1 * lax.rsqrt(ms + EPS) * g_ref[...]
    h2_ref[...] = h2.astype(BF16)
    route_ref[...] = _route(_dg3(h2, wr_ref[...]) + rb_ref[...])


def _out_proj(x, ya, yb, yc, wo_hi, wo_lo, g, wr, rb, *, tm=320):
    t = x.shape[0]
    row = lambda w: pl.BlockSpec((tm, w), lambda i: (i, 0))
    const = lambda r, c: _const_spec((r, c), 1)
    parts = lambda w: (w[0:D_A], w[D_A:D_A + D_B], w[D_A + D_B:])
    wspecs = [const(D_A, D_MODEL), const(D_B, D_MODEL), const(D_C, D_MODEL)]
    return pl.pallas_call(
        functools.partial(_out_proj_body, n_precise=ROW_TILE // tm),
        grid=(t // tm,),
        in_specs=[row(D_MODEL), row(D_A), row(D_B), row(D_C)] + wspecs + wspecs
        + [const(1, D_MODEL), const(D_MODEL, LANES), const(1, LANES)],
        out_specs=[row(D_MODEL), row(D_MODEL), row(LANES)],
        out_shape=[
            jax.ShapeDtypeStruct((t, D_MODEL), F32),
            jax.ShapeDtypeStruct((t, D_MODEL), BF16),
            jax.ShapeDtypeStruct((t, LANES), F32),
        ],
        compiler_params=_cparams("parallel"),
        name="out_proj",
    )(x, ya, yb, yc, *parts(wo_hi), *parts(wo_lo), g, wr, rb)


MOE_TM = 256


def _moe_body(te_ref, nt_ref, xs_ref, wg_ref, wu_ref, wd_ref, ys_ref, wg16, wu16, wd16):
    i = pl.program_id(0)
    changed = te_ref[i] != te_ref[jnp.maximum(i - 1, 0)]

    @pl.when((i == 0) | changed)
    def _():
        wg16[...] = wg_ref[0].astype(BF16)
        wu16[...] = wu_ref[0].astype(BF16)
        wd16[...] = wd_ref[0].astype(BF16)

    @pl.when(i < nt_ref[0])
    def _():
        x = xs_ref[...]
        act = _silu(_dot(x, wg16[...])) * _dot(x, wu16[...])
        ys_ref[...] = _dot(act.astype(BF16), wd16[...])

    @pl.when(i >= nt_ref[0])
    def _():
        ys_ref[...] = jnp.zeros(ys_ref.shape, F32)


def _moe_gemm(tile_expert, n_tiles, xs, wg, wu, wd, layer):
    p_rows = xs.shape[0]
    tm = MOE_TM
    grid_spec = pltpu.PrefetchScalarGridSpec(
        num_scalar_prefetch=2,
        grid=(p_rows // tm,),
        in_specs=[
            pl.BlockSpec((tm, D_MODEL), lambda i, te, nt: (i, 0)),
            pl.BlockSpec((None, 1, D_MODEL, D_FF), lambda i, te, nt: (layer, te[i], 0, 0)),
            pl.BlockSpec((None, 1, D_MODEL, D_FF), lambda i, te, nt: (layer, te[i], 0, 0)),
            pl.BlockSpec((None, 1, D_FF, D_MODEL), lambda i, te, nt: (layer, te[i], 0, 0)),
        ],
        out_specs=pl.BlockSpec((tm, D_MODEL), lambda i, te, nt: (i, 0)),
        scratch_shapes=[
            pltpu.VMEM((D_MODEL, D_FF), BF16),
            pltpu.VMEM((D_MODEL, D_FF), BF16),
            pltpu.VMEM((D_FF, D_MODEL), BF16),
        ],
    )
    return pl.pallas_call(
        _moe_body,
        grid_spec=grid_spec,
        out_shape=jax.ShapeDtypeStruct((p_rows, D_MODEL), F32),
        compiler_params=_cparams("arbitrary"),
        name="moe_gemm",
    )(tile_expert, n_tiles, xs, wg, wu, wd)


def _moe_plan(route, tm):
    t = route.shape[0]
    n_assign = 2 * t
    ids = route[:, 0:2].astype(jnp.int32).reshape(-1)
    order = jnp.argsort(ids, stable=True).astype(jnp.int32)
    counts = jnp.sum(ids[:, None] == jnp.arange(N_EXPERTS, dtype=jnp.int32)[None, :], axis=0).astype(jnp.int32)
    padded = ((counts + tm - 1) // tm) * tm
    pend = jnp.cumsum(padded)
    pstart = pend - padded
    start = jnp.cumsum(counts) - counts
    sorted_e = ids[order]
    dest = pstart[sorted_e] + (jnp.arange(n_assign, dtype=jnp.int32) - start[sorted_e])
    p_rows = n_assign + N_EXPERTS * tm
    p_rows = ((p_rows + tm - 1) // tm) * tm
    src_tok = jnp.zeros((p_rows,), jnp.int32).at[dest].set(order // 2)
    pos = jnp.zeros((n_assign,), jnp.int32).at[order].set(dest)
    n_tiles = (pend[-1] // tm).astype(jnp.int32)
    tile_start = jnp.arange(p_rows // tm, dtype=jnp.int32) * tm
    tile_e = jnp.sum(tile_start[:, None] >= pend[None, :], axis=1).astype(jnp.int32)
    last_e = jnp.sum((n_tiles - 1) * tm >= pend).astype(jnp.int32)
    tile_e = jnp.minimum(tile_e, last_e)
    return src_tok, pos.reshape(t, 2), tile_e, n_tiles.reshape(1)


def _moe_dense3_body(x1_ref, g_ref, route_ref, wg_ref, wu_ref, wd_ref, o_ref):
    e = pl.program_id(0)

    @pl.when(e == 0)
    def _():
        o_ref[...] = jnp.zeros(o_ref.shape, F32)

    x1 = x1_ref[...]
    ms = jnp.mean(x1 * x1, axis=-1, keepdims=True)
    h2 = x1 * lax.rsqrt(ms + EPS) * g_ref[...]
    route = route_ref[...]
    ef = e.astype(F32)
    gate = jnp.where(route[:, 0:1] == ef, route[:, 2:3], 0.0) + jnp.where(route[:, 1:2] == ef, route[:, 3:4], 0.0)
    act = _silu(_dg3(h2, wg_ref[0])) * _dg3(h2, wu_ref[0])
    o_ref[...] += gate * _dg3(act, wd_ref[0])


def _moe_dense3(x1, row0, ns, g, route, wg, wu, wd, layer):
    rows = pl.BlockSpec((ns, D_MODEL), lambda e: (row0 // ns, 0))
    return pl.pallas_call(
        _moe_dense3_body,
        grid=(N_EXPERTS,),
        in_specs=[
            rows,
            pl.BlockSpec((1, D_MODEL), lambda e: (0, 0)),
            pl.BlockSpec((ns, LANES), lambda e: (row0 // ns, 0)),
            pl.BlockSpec((None, 1, D_MODEL, D_FF), lambda e: (layer, e, 0, 0)),
            pl.BlockSpec((None, 1, D_MODEL, D_FF), lambda e: (layer, e, 0, 0)),
            pl.BlockSpec((None, 1, D_FF, D_MODEL), lambda e: (layer, e, 0, 0)),
        ],
        out_specs=pl.BlockSpec((ns, D_MODEL), lambda e: (0, 0)),
        out_shape=jax.ShapeDtypeStruct((ns, D_MODEL), F32),
        compiler_params=_cparams("arbitrary"),
        name="moe_dense3",
    )(x1, g, route, wg, wu, wd)


def _pad_lanes(v, n=LANES):
    return jnp.pad(v, (0, n - v.shape[0]))[None, :]


def _block_diag(w):
    h, d, _ = w.shape
    eye = jnp.eye(h, dtype=w.dtype)
    return (eye[:, None, :, None] * w[:, :, None, :]).reshape(h * d, h * d)


def _hi_lo(w):
    hi = lax.reduce_precision(w, exponent_bits=8, mantissa_bits=7)
    return hi.astype(BF16), (w - hi).astype(BF16)


def _pack_w_in(w):
    parts, start = [], 0
    for size in PROJ_SPLITS:
        parts.append(w[:, start:start + size])
        start += size
    xa, ga, z, xbc, dt, q, k, v, f = parts
    padc = lambda a: jnp.pad(a, ((0, 0), (0, LANES - a.shape[1])))
    return _hi_lo(jnp.concatenate([z, xbc, q, k, v, padc(dt), padc(f), xa, ga], axis=1))


def _pack_router(rg, rgb, re, reb):
    wr = jnp.concatenate([rg, jnp.transpose(re, (1, 0, 2)).reshape(D_MODEL, N_EXPERTS)], axis=1)
    wr = jnp.pad(wr, ((0, 0), (0, LANES - wr.shape[1])))
    rb = _pad_lanes(jnp.concatenate([rgb, reb.reshape(-1)]))
    return wr, rb


def kernel(x_prompt, x_sample, cache_k, cache_v, cache_logf, state_lru, state_lru_conv, state_ssm,
           state_ssm_conv, page_table, norm_mix, w_in, lru_conv_w, lru_conv_b, lru_w_a, lru_b_a, lru_w_x,
           lru_b_x, lru_lambda, ssm_conv_w, ssm_conv_b, ssm_dt_bias, ssm_a_log, ssm_d, ssm_norm,
           attn_q_norm, attn_k_norm, attn_f_bias, w_out, norm_ffn, router_group, router_group_bias,
           router_expert, router_expert_bias, moe_w_gate, moe_w_up, moe_w_down):
    nb, seq, _ = x_prompt.shape
    ns = x_sample.shape[0]
    depth = w_in.shape[0]
    n_pages = page_table.shape[1]
    tp = nb * seq
    bulk = seq - TAIL
    assert nb * TAIL + ns == ROW_TILE and (tp + ns) % ROW_TILE == 0 and bulk % TAIL == 0

    x = jnp.concatenate([x_prompt[:, :bulk].reshape(nb * bulk, D_MODEL),
                         x_prompt[:, bulk:].reshape(nb * TAIL, D_MODEL),
                         x_sample.reshape(ns, D_MODEL)], axis=0)
    kt_pool = jnp.transpose(cache_k, (0, 1, 3, 4, 2))
    vt_pool = jnp.transpose(cache_v, (0, 1, 3, 4, 2))
    lf_pool = jnp.transpose(cache_logf, (0, 3, 1, 2))[:, :, :, None, :]
    conv_lru = jnp.transpose(state_lru_conv, (0, 2, 1, 3))
    conv_ssm = jnp.transpose(state_ssm_conv, (0, 2, 1, 3))
    pt_flat = page_table.reshape(-1).astype(jnp.int32)

    outs_p, outs_s = [], []
    for l in range(depth):
        precise_tail = l < depth - 1
        wi_hi, wi_lo = _pack_w_in(w_in[l])
        u = _in_proj(x, norm_mix[l][None], wi_hi, wi_lo)

        lcb, ba, bx, lam = lru_conv_b[l][None], lru_b_a[l][None], lru_b_x[l][None], lru_lambda[l][None]
        wa, wx = _block_diag(lru_w_a[l]), _block_diag(lru_w_x[l])
        scb = ssm_conv_b[l][None]
        dtb, alog = _pad_lanes(ssm_dt_bias[l]), _pad_lanes(ssm_a_log[l])
        ng = ssm_norm[l][None]
        gq = jnp.tile(attn_q_norm[l], H_C)[None]
        gk = jnp.tile(attn_k_norm[l], H_C)[None]
        fb = _pad_lanes(attn_f_bias[l])

        ya_p, lru_h_p, lru_conv_p = _lru_prompt(u, nb, seq, lru_conv_w[l], lcb, wa, ba, wx, bx, lam)
        yb_p, ssm_h_p, ssm_conv_p = _ssd_prompt(u, nb, seq, ssm_conv_w[l], scb, dtb, alog, _pad_lanes(ssm_d[l]), ng,
                                                precise_tail=precise_tail)
        qh, ql, k_p, kh, kl, v_p, vh, vl, lf_p, ct = _qk_prep_prompt(u, nb, seq, gq, gk, fb)
        ct_pairs = jnp.pad(ct[:, 0:H_C].reshape(nb, H_C // 2, 2, seq), ((0, 0), (0, 0), (0, SUBLANES - 2), (0, 0)))
        yc_p = _fox_prompt(qh, ql, kh, kl, vh, vl, ct_pairs, nb, seq, precise_tail=precise_tail)

        (ya_s, lru_h_s, lru_conv_s, ssm_conv_s, dtxt, dat, bm, cm, xd, zg) = _sample_pre(
            u, tp, ns, state_lru[l], conv_lru[l], conv_ssm[l], lru_conv_w[l], lcb, wa, ba, wx, bx, lam,
            ssm_conv_w[l], scb, dtb, alog, jnp.repeat(ssm_d[l], HEAD_DIM)[None])
        ssm_h_s, yb_s = _ssd_update(state_ssm, l, dtxt, dat, bm, cm, xd, zg, ng)
        q_s, k_s, lf_s = _qk_prep_sample(u, tp, ns, gq, gk, fb)
        v_s = u[tp:, OFF_V:OFF_V + D_C]
        qkv_t = jnp.concatenate([q_s.T, k_s.T, v_s.T, lf_s[:, 0:2 * SUBLANES].T], axis=0)
        yc_s = _paged_attn(pt_flat, qkv_t, kt_pool, vt_pool, lf_pool, l, ns, n_pages).T

        ya = jnp.concatenate([ya_p, ya_s], axis=0)
        yb = jnp.concatenate([yb_p, yb_s], axis=0)
        yc = jnp.concatenate([yc_p, yc_s], axis=0)
        wo_hi, wo_lo = _hi_lo(w_out[l])
        wr, rb = _pack_router(router_group[l], router_group_bias[l], router_expert[l], router_expert_bias[l])
        x1, h2, route = _out_proj(x, ya, yb, yc, wo_hi, wo_lo, norm_ffn[l][None], wr, rb)

        src_tok, pos, tile_e, n_tiles = _moe_plan(route, MOE_TM)
        ys = _moe_gemm(tile_e, n_tiles, h2[src_tok], moe_w_gate, moe_w_up, moe_w_down, l)
        x = x1 + route[:, 2:3] * ys[pos[:, 0]] + route[:, 3:4] * ys[pos[:, 1]]
        if l < depth - 1:
            moe_s = _moe_dense3(x1, tp, ns, norm_ffn[l][None], route, moe_w_gate, moe_w_up, moe_w_down, l)
            x = jnp.concatenate([x[:tp], x1[tp:] + moe_s], axis=0)

        outs_p.append((
            k_p.reshape(nb, seq, H_C, HEAD_DIM), v_p.reshape(nb, seq, H_C, HEAD_DIM),
            lf_p[:, 0:H_C].reshape(nb, seq, H_C), lru_h_p[:, 0], lru_conv_p, ssm_h_p, ssm_conv_p))
        outs_s.append((
            k_s.reshape(ns, 1, H_C, HEAD_DIM), v_s.reshape(ns, 1, H_C, HEAD_DIM),
            lf_s[:, 0:H_C].reshape(ns, 1, H_C), lru_h_s, jnp.transpose(lru_conv_s, (1, 0, 2)),
            ssm_h_s, jnp.transpose(ssm_conv_s, (1, 0, 2))))

    y_prompt = jnp.concatenate([x[:nb * bulk].reshape(nb, bulk, D_MODEL),
                                x[nb * bulk:tp].reshape(nb, TAIL, D_MODEL)], axis=1)
    stack = lambda states, j: jnp.stack([s[j] for s in states], axis=0)
    return (y_prompt, x[tp:].reshape(ns, 1, D_MODEL),
            *[stack(outs_p, j) for j in range(7)], *[stack(outs_s, j) for j in range(7)])
```

```python
import functools
import math

import jax
import jax.numpy as jnp
from jax import lax
from jax.experimental import pallas as pl
from jax.experimental.pallas import tpu as pltpu

F32 = jnp.float32
BF16 = jnp.bfloat16

D_MODEL = 2048
HEAD_DIM = 64
D_A = 512
H_A = 8
D_B = 768
H_B = 12
G_B = 4
N_B = 128
CONV_B = D_B + 2 * G_B * N_B
D_C = 768
H_C = 12
CONV_W = 4
LRU_C = 8.0
SSD_CHUNK = 128
PAGE_SIZE = 128
ATTN_SCALE = 1.0 / math.sqrt(HEAD_DIM)
PROJ_SPLITS = (D_A, D_A, D_B, CONV_B, H_B, D_C, D_C, D_C, H_C)
N_GROUPS = 4
E_PER_GROUP = 4
N_EXPERTS = 16
D_FF = 512
EPS = 1e-6
NEG_BIG = -1e30

LANES = 128
SUBLANES = 8
VMEM_LIMIT = 56 * 1024 * 1024

TAIL = 256
LOG2E = math.log2(math.e)

OFF_Z = 0
OFF_XBC = 768
OFF_Q = 2560
OFF_K = 3328
OFF_V = 4096
OFF_DT = 4864
OFF_F = 4992
OFF_XA = 5120
OFF_GA = 5632
N_U = 6144


def _cparams(*sem):
    return pltpu.CompilerParams(dimension_semantics=sem, vmem_limit_bytes=VMEM_LIMIT)


_NN = (((1,), (0,)), ((), ()))
_NT = (((1,), (1,)), ((), ()))
_TN = (((0,), (0,)), ((), ()))


def _dg(a, b, dims=_NN):
    return lax.dot_general(a, b, dims, preferred_element_type=F32)


def _dot(a, b):
    return _dg(a, b, _NN)


def _split2(x):
    hi = x.astype(BF16)
    lo = (x - hi.astype(F32)).astype(BF16)
    return hi, lo


def _split3(x):
    hi = x.astype(BF16)
    r = x - hi.astype(F32)
    mid = r.astype(BF16)
    lo = (r - mid.astype(F32)).astype(BF16)
    return hi, mid, lo


def _dot_sel_r(x, sel):
    hi, mid, lo = _split3(x)
    return _dot(hi, sel) + _dot(mid, sel) + _dot(lo, sel)


def _dot_sel_l(sel, x):
    hi, mid, lo = _split3(x)
    return _dot(sel, hi) + _dot(sel, mid) + _dot(sel, lo)


def _dg3(a, b, dims=_NN):
    ah, al = _split2(a)
    bh, bl = _split2(b)
    return _dg(ah, bh, dims) + _dg(al, bh, dims) + _dg(ah, bl, dims)


def _mm(a, b, dims, precise):
    if precise:
        return _dg3(a, b, dims)
    return _dg(a.astype(BF16), b.astype(BF16), dims)


def _sigmoid(x):
    return 1.0 / (1.0 + jnp.exp(-x))


def _silu(x):
    return x * _sigmoid(x)


def _log_sigmoid(x):
    return jnp.minimum(x, 0.0) - jnp.log1p(jnp.exp(-jnp.abs(x)))


def _softplus(x):
    return jnp.maximum(x, 0.0) + jnp.log1p(jnp.exp(-jnp.abs(x)))


def _gelu_tanh(x):
    return 0.5 * x * (1.0 + jnp.tanh(math.sqrt(2.0 / math.pi) * (x + 0.044715 * (x * x * x))))


def _iota(shape, dim):
    return lax.broadcasted_iota(jnp.int32, shape, dim)


def _seg_matrix(n_rows, n_cols, seg, transpose=False):
    if transpose:
        m = (_iota((n_cols, n_rows), 1) // seg) == _iota((n_cols, n_rows), 0)
    else:
        m = (_iota((n_rows, n_cols), 0) // seg) == _iota((n_rows, n_cols), 1)
    return jnp.where(m, 1.0, 0.0).astype(BF16)


def _precise_tiles(tm, nb, seq, ns):
    spans = [((b + 1) * seq - TAIL, (b + 1) * seq) for b in range(nb)] + [(nb * seq, nb * seq + ns)]
    n_tiles = (nb * seq + ns) // tm
    return tuple(i for i in range(n_tiles) if any(lo < (i + 1) * tm and hi > i * tm for lo, hi in spans))


def _is_in(i, ids):
    hit = i == ids[0]
    for t in ids[1:]:
        hit = hit | (i == t)
    return hit


def _const_spec(shape, n_grid):
    zeros = (0,) * len(shape)
    return pl.BlockSpec(shape, lambda *_: zeros, pipeline_mode=pl.Buffered(1))


def _in_proj_body(x_ref, g_ref, wh_ref, wl_ref, o_ref, xh_ref, xl_ref, *, precise_tiles):
    precise = _is_in(pl.program_id(0), precise_tiles)

    @pl.when(pl.program_id(1) == 0)
    def _():
        x = x_ref[...]
        ms = jnp.mean(x * x, axis=-1, keepdims=True)
        xh_ref[...], xl_ref[...] = _split2(x * lax.rsqrt(ms + EPS) * g_ref[...])

    @pl.when(jnp.logical_not(precise))
    def _():
        o_ref[...] = _dot(xh_ref[...], wh_ref[...])

    @pl.when(precise)
    def _():
        o_ref[...] = (_dot(xh_ref[...], wh_ref[...]) + _dot(xl_ref[...], wh_ref[...])
                      + _dot(xh_ref[...], wl_ref[...]))


def _in_proj(x, g, wh, wl, precise_tiles, *, tm, tn=1024):
    t = x.shape[0]
    return pl.pallas_call(
        functools.partial(_in_proj_body, precise_tiles=precise_tiles),
        grid=(t // tm, N_U // tn),
        in_specs=[
            pl.BlockSpec((tm, D_MODEL), lambda i, j: (i, 0)),
            pl.BlockSpec((1, D_MODEL), lambda i, j: (0, 0)),
            pl.BlockSpec((D_MODEL, tn), lambda i, j: (0, j)),
            pl.BlockSpec((D_MODEL, tn), lambda i, j: (0, jnp.where(_is_in(i, precise_tiles), j, 0))),
        ],
        out_specs=pl.BlockSpec((tm, tn), lambda i, j: (i, j)),
        out_shape=jax.ShapeDtypeStruct((t, N_U), F32),
        scratch_shapes=[pltpu.VMEM((tm, D_MODEL), BF16), pltpu.VMEM((tm, D_MODEL), BF16)],
        compiler_params=_cparams("parallel", "arbitrary"),
        name="in_proj",
    )(x, g, wh, wl)


def _lru_gates(xc, wa, ba, wx, bx, lam):
    r = _sigmoid(_dg3(xc, wa) + ba)
    i = _sigmoid(_dg3(xc, wx) + bx)
    log_a = LRU_C * r * _log_sigmoid(lam)
    a = jnp.exp(log_a)
    th = jnp.tanh(log_a)
    mult = jnp.sqrt(-2.0 * th / (1.0 - th))
    return a, mult * (i * xc)


def _lru_prompt_body(u_ref, cw_ref, cb_ref, wa_ref, ba_ref, wx_ref, bx_ref, lam_ref,
                     y_ref, h_out_ref, conv_out_ref, xp_ref, h_ref, *, tl):
    i = pl.program_id(1)

    @pl.when(i == 0)
    def _():
        xp_ref[0:SUBLANES, :] = jnp.zeros((SUBLANES, D_A), F32)
        h_ref[...] = jnp.zeros((1, D_A), F32)

    x = u_ref[:, 0:D_A]
    ga = u_ref[:, D_A:2 * D_A]
    xp_ref[SUBLANES:SUBLANES + tl, :] = x
    cw = cw_ref[...]
    xc = cb_ref[...] + cw[3:4, :] * x
    for j in range(CONV_W - 1):
        xc = xc + cw[j:j + 1, :] * xp_ref[pl.ds(SUBLANES - 3 + j, tl), :]
    xp_ref[0:SUBLANES, :] = x[tl - SUBLANES:tl, :]

    a, u = _lru_gates(xc, wa_ref[...], ba_ref[...], wx_ref[...], bx_ref[...], lam_ref[...])

    row = _iota((tl, D_A), 0)
    s = 1
    while s < tl:
        keep = row >= s
        a_sh = jnp.where(keep, pltpu.roll(a, s, axis=0), 1.0)
        u_sh = jnp.where(keep, pltpu.roll(u, s, axis=0), 0.0)
        u = u + a * u_sh
        a = a * a_sh
        s *= 2
    h = a * h_ref[...] + u
    h_ref[...] = h[tl - 1:tl, :]
    y_ref[...] = h * _gelu_tanh(ga)

    @pl.when(i == pl.num_programs(1) - 1)
    def _():
        h_out_ref[0] = h[tl - 1:tl, :]
        conv_out_ref[0] = x[tl - 3:tl, :]


def _lru_prompt(u, nb, seq, cw, cb, wa, ba, wx, bx, lam, *, tl=512):
    nt = seq // tl
    vec = lambda n: pl.BlockSpec((1, n), lambda b, i: (0, 0))
    mat = lambda r, c: pl.BlockSpec((r, c), lambda b, i: (0, 0))
    rb = lambda b, i: b * nt + i
    return pl.pallas_call(
        functools.partial(_lru_prompt_body, tl=tl),
        grid=(nb, nt),
        in_specs=[
            pl.BlockSpec((tl, 2 * D_A), lambda b, i: (rb(b, i), OFF_XA // (2 * D_A))),
            mat(CONV_W, D_A), vec(D_A), mat(D_A, D_A), vec(D_A), mat(D_A, D_A), vec(D_A), vec(D_A),
        ],
        out_specs=[
            pl.BlockSpec((tl, D_A), lambda b, i: (rb(b, i), 0)),
            pl.BlockSpec((1, 1, D_A), lambda b, i: (b, 0, 0)),
            pl.BlockSpec((1, CONV_W - 1, D_A), lambda b, i: (b, 0, 0)),
        ],
        out_shape=[
            jax.ShapeDtypeStruct((nb * seq, D_A), F32),
            jax.ShapeDtypeStruct((nb, 1, D_A), F32),
            jax.ShapeDtypeStruct((nb, CONV_W - 1, D_A), F32),
        ],
        scratch_shapes=[pltpu.VMEM((SUBLANES + tl, D_A), F32), pltpu.VMEM((1, D_A), F32)],
        compiler_params=_cparams("parallel", "arbitrary"),
        name="lru_prompt",
    )(u, cw, cb, wa, ba, wx, bx, lam)


def _group_rmsnorm(y, g):
    gw = D_B // G_B
    ss = _dot_sel_r(y * y, _seg_matrix(D_B, LANES, gw))
    rs = lax.rsqrt(ss * (1.0 / gw) + EPS)
    return y * _dot_sel_r(rs, _seg_matrix(D_B, LANES, gw, transpose=True)) * g


def _ssd_chunk(x, bm, cm, dt, dta, dsk, h_ref, precise):
    q = SSD_CHUNK
    tri = jnp.where(_iota((q, q), 0) >= _iota((q, q), 1), 1.0, 0.0).astype(BF16)
    cum = _dot_sel_l(tri, dta)
    cum_t = cum.T
    dt_t = dt.T
    causal = _iota((q, q), 0) >= _iota((q, q), 1)
    ys = []
    for g in range(G_B):
        bg = bm[:, g * N_B:(g + 1) * N_B]
        cg = cm[:, g * N_B:(g + 1) * N_B]
        scores = _mm(cg, bg, _NT, precise)
        for r in range(H_B // G_B):
            h = g * (H_B // G_B) + r
            xh = x[:, h * HEAD_DIM:(h + 1) * HEAD_DIM]
            cl = cum[:, h:h + 1]
            cs = cum_t[h:h + 1, :]
            c_last = cum_t[h:h + 1, q - 1:q]
            decay = jnp.exp(jnp.where(causal, cl - cs, NEG_BIG))
            w = scores * decay * dt_t[h:h + 1, :]
            y_diag = _mm(w, xh, _NN, precise)
            h_prev = h_ref[h]
            y_off = _mm(cg, h_prev, _NT, precise) * jnp.exp(cl)
            de = jnp.exp(c_last - cl) * dt[:, h:h + 1]
            st = _mm(xh * de, bg, _TN, precise)
            h_ref[h] = jnp.exp(c_last) * h_prev + st
            ys.append(y_diag + y_off + dsk[:, h:h + 1] * xh)
    return jnp.concatenate(ys, axis=1)


def _ssd_prompt_body(u_ref, s_ref, cw_ref, cb_ref, dtb_ref, alog_ref, dsk_ref, ng_ref,
                     y_ref, h_out_ref, conv_out_ref, xp_ref, h_ref, *, precise_tail):
    c = pl.program_id(1)
    nc = pl.num_programs(1)
    q = SSD_CHUNK

    @pl.when(c == 0)
    def _():
        xp_ref[0:SUBLANES, :] = jnp.zeros((SUBLANES, CONV_B), F32)
        h_ref[...] = jnp.zeros((H_B, HEAD_DIM, N_B), F32)

    z = u_ref[:, OFF_Z:OFF_Z + D_B]
    xbc = u_ref[:, OFF_XBC:OFF_XBC + CONV_B]
    xp_ref[SUBLANES:SUBLANES + q, :] = xbc
    cw = cw_ref[...]
    xc = cb_ref[...] + cw[3:4, :] * xbc
    for j in range(CONV_W - 1):
        xc = xc + cw[j:j + 1, :] * xp_ref[pl.ds(SUBLANES - 3 + j, q), :]
    xp_ref[0:SUBLANES, :] = xbc[q - SUBLANES:q, :]
    xc = _silu(xc)
    x = xc[:, 0:D_B]
    bm = xc[:, D_B:D_B + G_B * N_B]
    cm = xc[:, D_B + G_B * N_B:]
    dt = _softplus(s_ref[:, 0:LANES] + dtb_ref[...])
    dta = dt * (-jnp.exp(alog_ref[...]))
    gate = _silu(z)

    def finish(precise):
        y = _ssd_chunk(x, bm, cm, dt, dta, dsk_ref[...], h_ref, precise) * gate
        y_ref[...] = _group_rmsnorm(y, ng_ref[...])

    if precise_tail:
        in_tail = c >= nc - TAIL // q
        pl.when(in_tail)(lambda: finish(True))
        pl.when(jnp.logical_not(in_tail))(lambda: finish(False))
    else:
        finish(False)

    @pl.when(c == nc - 1)
    def _():
        h_out_ref[0] = h_ref[...]
        conv_out_ref[0] = xbc[q - 3:q, :]


def _ssd_prompt(u, nb, seq, cw, cb, dtb, alog, dsk, ng, *, precise_tail):
    nc = seq // SSD_CHUNK
    q = SSD_CHUNK
    vec = lambda n: pl.BlockSpec((1, n), lambda b, c: (0, 0))
    rb = lambda b, c: b * nc + c
    return pl.pallas_call(
        functools.partial(_ssd_prompt_body, precise_tail=precise_tail),
        grid=(nb, nc),
        in_specs=[
            pl.BlockSpec((q, OFF_Q), lambda b, c: (rb(b, c), 0)),
            pl.BlockSpec((q, 2 * LANES), lambda b, c: (rb(b, c), OFF_DT // (2 * LANES))),
            pl.BlockSpec((CONV_W, CONV_B), lambda b, c: (0, 0)),
            vec(CONV_B), vec(LANES), vec(LANES), vec(LANES), vec(D_B),
        ],
        out_specs=[
            pl.BlockSpec((q, D_B), lambda b, c: (rb(b, c), 0)),
            pl.BlockSpec((1, H_B, HEAD_DIM, N_B), lambda b, c: (b, 0, 0, 0)),
            pl.BlockSpec((1, CONV_W - 1, CONV_B), lambda b, c: (b, 0, 0)),
        ],
        out_shape=[
            jax.ShapeDtypeStruct((nb * seq, D_B), F32),
            jax.ShapeDtypeStruct((nb, H_B, HEAD_DIM, N_B), F32),
            jax.ShapeDtypeStruct((nb, CONV_W - 1, CONV_B), F32),
        ],
        scratch_shapes=[pltpu.VMEM((SUBLANES + q, CONV_B), F32),
                        pltpu.VMEM((H_B, HEAD_DIM, N_B), F32)],
        compiler_params=_cparams("parallel", "arbitrary"),
        name="ssd_prompt",
    )(u, u, cw, cb, dtb, alog, dsk, ng)


def _head_rmsnorm(x, g):
    ss = _dot_sel_r(x * x, _seg_matrix(D_C, LANES, HEAD_DIM))
    rs = lax.rsqrt(ss * (1.0 / HEAD_DIM) + EPS)
    return x * _dot_sel_r(rs, _seg_matrix(D_C, LANES, HEAD_DIM, transpose=True)) * g


D_CX = H_C * LANES


def _widen_heads(x, extras):
    parts = []
    for h in range(H_C):
        parts += [x[:, h * HEAD_DIM:(h + 1) * HEAD_DIM], extras[h]]
    return jnp.concatenate(parts, axis=1)


def _qk_prep_prompt_body(u_ref, gq_ref, gk_ref, fb_ref, k_ref, v_ref, lf_ref, qh_ref, kh_ref, vh_ref,
                         *rest, tl, with_lo):
    if with_lo:
        ql_ref, kl_ref, vl_ref, carry_ref = rest
    else:
        (carry_ref,) = rest
    q = u_ref[:, 0:D_C]
    k = u_ref[:, D_C:2 * D_C]
    v = u_ref[:, 2 * D_C:3 * D_C]
    f_raw = u_ref[:, OFF_F - OFF_Q:OFF_F - OFF_Q + LANES]
    qn = _head_rmsnorm(q, gq_ref[...]) * (ATTN_SCALE * LOG2E)
    kn = _head_rmsnorm(k, gk_ref[...])
    k_ref[...] = kn
    v_ref[...] = v
    lf = _log_sigmoid(f_raw + fb_ref[...])
    lf_ref[...] = lf

    @pl.when(pl.program_id(1) == 0)
    def _():
        carry_ref[...] = jnp.zeros((1, LANES), F32)

    tri = jnp.where(_iota((tl, tl), 0) >= _iota((tl, tl), 1), 1.0, 0.0).astype(BF16)
    c = _dot_sel_l(tri, lf) + carry_ref[...]
    carry_ref[...] = c[tl - 1:tl, :]
    b3 = [p.astype(F32) for p in _split3(c * (-LOG2E))]

    lane = _iota((tl, HEAD_DIM), 1)
    zeros = jnp.zeros((tl, HEAD_DIM), F32)
    ones = jnp.ones((tl, HEAD_DIM), F32)
    sel3 = jnp.where(lane < 3, 1.0, 0.0)
    bias = [jnp.where(lane == 0, b3[0][:, h:h + 1],
                      jnp.where(lane == 1, b3[1][:, h:h + 1],
                                jnp.where(lane == 2, b3[2][:, h:h + 1], 0.0))) for h in range(H_C)]

    def hi_lo(x):
        hi = x.astype(BF16)
        return hi, x - hi.astype(F32)

    q_hi, q_rem = hi_lo(qn)
    k_hi, k_rem = hi_lo(kn)
    v_hi, v_rem = hi_lo(v)
    qh_ref[...] = _widen_heads(q_hi.astype(F32), [sel3] * H_C).astype(BF16)
    kh_ref[...] = _widen_heads(k_hi.astype(F32), bias).astype(BF16)
    vh_ref[...] = _widen_heads(v_hi.astype(F32), [ones] * H_C).astype(BF16)
    if with_lo:
        ql_ref[...] = _widen_heads(q_rem, [zeros] * H_C).astype(BF16)
        kl_ref[...] = _widen_heads(k_rem, [zeros] * H_C).astype(BF16)
        vl_ref[...] = _widen_heads(v_rem, [zeros] * H_C).astype(BF16)


def _qk_prep_prompt(u, nb, seq, gq, gk, fb, *, with_lo, tl=512):
    nt = seq // tl
    rows = nb * seq
    wq = OFF_XA - OFF_Q
    vec = lambda n: pl.BlockSpec((1, n), lambda b, i: (0, 0))
    nat = lambda w: pl.BlockSpec((tl, w), lambda b, i: (b * nt + i, 0))
    sx = jax.ShapeDtypeStruct((rows, D_CX), BF16)
    s32 = jax.ShapeDtypeStruct((rows, D_C), F32)
    n_x = 6 if with_lo else 3
    return pl.pallas_call(
        functools.partial(_qk_prep_prompt_body, tl=tl, with_lo=with_lo),
        grid=(nb, nt),
        in_specs=[
            pl.BlockSpec((tl, wq), lambda b, i: (b * nt + i, OFF_Q // wq)),
            vec(D_C), vec(D_C), vec(LANES),
        ],
        out_specs=[nat(D_C), nat(D_C), nat(LANES)] + [nat(D_CX)] * n_x,
        out_shape=[s32, s32, jax.ShapeDtypeStruct((rows, LANES), F32)] + [sx] * n_x,
        scratch_shapes=[pltpu.VMEM((1, LANES), F32)],
        compiler_params=_cparams("parallel", "arbitrary"),
        name="qk_prep_prompt",
    )(u, gq, gk, fb)


def _qk_prep_sample_body(u_ref, gq_ref, gk_ref, fb_ref, q_ref, k_ref, lf_ref):
    q = u_ref[:, 0:D_C]
    k = u_ref[:, D_C:2 * D_C]
    f_raw = u_ref[:, OFF_F - OFF_Q:OFF_F - OFF_Q + LANES]
    q_ref[...] = _head_rmsnorm(q, gq_ref[...]) * ATTN_SCALE
    k_ref[...] = _head_rmsnorm(k, gk_ref[...])
    lf_ref[...] = _log_sigmoid(f_raw + fb_ref[...])


def _qk_prep_sample(u, row0, ns, gq, gk, fb):
    wq = OFF_XA - OFF_Q
    full = lambda r, c: pl.BlockSpec((r, c), lambda i: (0, 0))
    return pl.pallas_call(
        _qk_prep_sample_body,
        grid=(1,),
        in_specs=[pl.BlockSpec((ns, wq), lambda i: (row0 // ns, OFF_Q // wq)),
                  full(1, D_C), full(1, D_C), full(1, LANES)],
        out_specs=[full(ns, D_C), full(ns, D_C), full(ns, LANES)],
        out_shape=[jax.ShapeDtypeStruct((ns, D_C), F32), jax.ShapeDtypeStruct((ns, D_C), F32),
                   jax.ShapeDtypeStruct((ns, LANES), F32)],
        compiler_params=_cparams("arbitrary"),
        name="qk_prep_sample",
    )(u, gq, gk, fb)


def _fox_tile(i, refs, tq, precise):
    if precise:
        qh_ref, kh_ref, vh_ref, ql_ref, kl_ref, vl_ref, o_ref = refs
    else:
        qh_ref, kh_ref, vh_ref, o_ref = refs
    causal = _iota((tq, tq), 0) >= _iota((tq, tq), 1)
    heads = [slice(hh * LANES, (hh + 1) * LANES) for hh in range(2)]
    qh = [qh_ref[:, sl] for sl in heads]
    ql = [ql_ref[:, sl] for sl in heads] if precise else None

    def step(j, carry, masked):
        off = pl.multiple_of(j * tq, tq)
        out = []
        for hh, sl in enumerate(heads):
            m, acc = carry[hh]
            kh = kh_ref[pl.ds(off, tq), sl]
            vh = vh_ref[pl.ds(off, tq), sl]
            s = _dg(qh[hh], kh, _NT)
            if precise:
                s = s + _dg(ql[hh], kh, _NT) + _dg(qh[hh], kl_ref[pl.ds(off, tq), sl], _NT)
            if masked:
                s = jnp.where(causal, s, NEG_BIG)
            m_new = jnp.maximum(m, jnp.max(s, axis=1, keepdims=True))
            alpha = jnp.exp2(m - m_new)
            p = jnp.exp2(s - m_new)
            if precise:
                p_hi, p_lo = _split2(p)
                pv = _dot(p_hi, vh) + _dot(p_lo, vh) + _dot(p_hi, vl_ref[pl.ds(off, tq), sl])
            else:
                pv = _dot(p.astype(BF16), vh)
            out.append((m_new, alpha * acc + pv))
        return tuple(out)

    init = tuple((jnp.full((tq, 1), NEG_BIG, F32), jnp.zeros((tq, LANES), F32)) for _ in heads)
    carry = lax.fori_loop(0, i, functools.partial(step, masked=False), init)
    carry = step(i, carry, True)
    o = [acc / pltpu.roll(acc, HEAD_DIM, axis=1) for _, acc in carry]
    lane = _iota((tq, LANES), 1)
    o_ref[...] = jnp.where(lane < HEAD_DIM, o[0], pltpu.roll(o[1], HEAD_DIM, axis=1))


def _fox_body(*refs, tq, precise_tail):
    i = pl.program_id(2)
    if precise_tail:
        last = i == pl.num_programs(2) - 1
        plain = refs[0:3] + refs[6:]
        pl.when(last)(lambda: _fox_tile(i, refs, tq, True))
        pl.when(jnp.logical_not(last))(lambda: _fox_tile(i, plain, tq, False))
    else:
        _fox_tile(i, refs, tq, False)


def _fox_prompt(operands, nb, seq, *, precise_tail, tq=512):
    nq = seq // tq
    npair = H_C // 2
    qspec = pl.BlockSpec((tq, 2 * LANES), lambda b, p, i: (b * nq + i, p))
    kspec = pl.BlockSpec((seq, 2 * LANES), lambda b, p, i: (b, p))
    return pl.pallas_call(
        functools.partial(_fox_body, tq=tq, precise_tail=precise_tail),
        grid=(nb, npair, nq),
        in_specs=[qspec, kspec, kspec] * (2 if precise_tail else 1),
        out_specs=pl.BlockSpec((tq, LANES), lambda b, p, i: (b * nq + i, p)),
        out_shape=jax.ShapeDtypeStruct((nb * seq, D_C), F32),
        compiler_params=_cparams("parallel", "parallel", "arbitrary"),
        name="fox_prompt",
    )(*operands)


def _sample_pre_body(u_ref, hl_ref, cl_ref, cs_ref, lcw_ref, lcb_ref, wa_ref, ba_ref, wx_ref, bx_ref,
                     lam_ref, scw_ref, scb_ref, dtb_ref, alog_ref, dsk_ref,
                     ya_ref, hnew_ref, lconv_ref, sconv_ref, dtxt_ref, dat_ref, bm_ref, cm_ref,
                     xd_ref, zg_ref):
    xa = u_ref[:, OFF_XA:OFF_XA + D_A]
    ga = u_ref[:, OFF_GA:OFF_GA + D_A]
    cw = lcw_ref[...]
    xc = lcb_ref[...] + cw[3:4, :] * xa
    for j in range(CONV_W - 1):
        xc = xc + cw[j:j + 1, :] * cl_ref[j]
    lconv_ref[0] = cl_ref[1]
    lconv_ref[1] = cl_ref[2]
    lconv_ref[2] = xa
    a, uu = _lru_gates(xc, wa_ref[...], ba_ref[...], wx_ref[...], bx_ref[...], lam_ref[...])
    h = a * hl_ref[...] + uu
    hnew_ref[...] = h
    ya_ref[...] = h * _gelu_tanh(ga)

    z = u_ref[:, OFF_Z:OFF_Z + D_B]
    xbc = u_ref[:, OFF_XBC:OFF_XBC + CONV_B]
    cw = scw_ref[...]
    xs = scb_ref[...] + cw[3:4, :] * xbc
    for j in range(CONV_W - 1):
        xs = xs + cw[j:j + 1, :] * cs_ref[j]
    sconv_ref[0] = cs_ref[1]
    sconv_ref[1] = cs_ref[2]
    sconv_ref[2] = xbc
    xs = _silu(xs)
    x = xs[:, 0:D_B]
    bm_ref[...] = xs[:, D_B:D_B + G_B * N_B]
    cm_ref[...] = xs[:, D_B + G_B * N_B:]
    dt = _softplus(u_ref[:, OFF_DT:OFF_DT + LANES] + dtb_ref[...])
    da = jnp.exp(dt * (-jnp.exp(alog_ref[...])))
    expand = _seg_matrix(D_B, LANES, HEAD_DIM, transpose=True)
    dtxt_ref[...] = (_dot_sel_r(dt, expand) * x).T
    dat_ref[...] = _dot_sel_r(da, expand).T
    xd_ref[...] = dsk_ref[...] * x
    zg_ref[...] = _silu(z)


def _sample_pre(u, row0, ns, h_lru, conv_lru, conv_ssm, lcw, lcb, wa, ba, wx, bx, lam, scw, scb, dtb, alog, dsk_e):
    full = lambda *shape: pl.BlockSpec(shape, lambda i: (0,) * len(shape))
    f = lambda *shape: jax.ShapeDtypeStruct(shape, F32)
    return pl.pallas_call(
        _sample_pre_body,
        grid=(1,),
        in_specs=[
            pl.BlockSpec((ns, N_U), lambda i: (row0 // ns, 0)),
            full(ns, D_A), full(CONV_W - 1, ns, D_A), full(CONV_W - 1, ns, CONV_B),
            full(CONV_W, D_A), full(1, D_A), full(D_A, D_A), full(1, D_A), full(D_A, D_A), full(1, D_A),
            full(1, D_A), full(CONV_W, CONV_B), full(1, CONV_B), full(1, LANES), full(1, LANES), full(1, D_B),
        ],
        out_specs=[
            full(ns, D_A), full(ns, D_A), full(CONV_W - 1, ns, D_A), full(CONV_W - 1, ns, CONV_B),
            full(D_B, ns), full(D_B, ns), full(ns, G_B * N_B), full(ns, G_B * N_B), full(ns, D_B), full(ns, D_B),
        ],
        out_shape=[
            f(ns, D_A), f(ns, D_A), f(CONV_W - 1, ns, D_A), f(CONV_W - 1, ns, CONV_B),
            f(D_B, ns), f(D_B, ns), f(ns, G_B * N_B), f(ns, G_B * N_B), f(ns, D_B), f(ns, D_B),
        ],
        compiler_params=_cparams("arbitrary"),
        name="sample_pre",
    )(u, h_lru, conv_lru, conv_ssm, lcw, lcb, wa, ba, wx, bx, lam, scw, scb, dtb, alog, dsk_e)


def _lane_bcast_column(parts, b, ns):
    onehot = jnp.where(_iota((ns, LANES), 0) == b, 1.0, 0.0).astype(BF16)
    out = _dot(parts[0], onehot)
    for p in parts[1:]:
        out = out + _dot(p, onehot)
    return out


def _ssd_update_body(st_ref, dtxt_ref, dat_ref, bm_ref, cm_ref, xd_ref, zg_ref, ng_ref,
                     st_out_ref, y_ref, yt_ref, *, tb, ns):
    i = pl.program_id(0)

    @pl.when(i == 0)
    def _():
        yt_ref[...] = jnp.zeros((D_B, ns), F32)

    dtx3 = _split3(dtxt_ref[...])
    da3 = _split3(dat_ref[...])
    lane = _iota((D_B, ns), 1)
    rows_per_group = D_B // G_B

    def group_rows(row):
        return jnp.concatenate(
            [jnp.broadcast_to(row[:, g * N_B:(g + 1) * N_B], (rows_per_group, N_B)) for g in range(G_B)], axis=0)

    for t in range(tb):
        b = i * tb + t
        xb = _lane_bcast_column(dtx3, b, ns)
        dab = _lane_bcast_column(da3, b, ns)
        bexp = group_rows(bm_ref[pl.ds(b, 1), :])
        cexp = group_rows(cm_ref[pl.ds(b, 1), :])
        s_new = dab * st_ref[t].reshape(D_B, N_B) + xb * bexp
        st_out_ref[t] = s_new.reshape(H_B, HEAD_DIM, N_B)
        ycol = jnp.sum(s_new * cexp, axis=1, keepdims=True)
        yt_ref[...] = jnp.where(lane == b, ycol, yt_ref[...])

    @pl.when(i == pl.num_programs(0) - 1)
    def _():
        y = (yt_ref[...].T + xd_ref[...]) * zg_ref[...]
        y_ref[...] = _group_rmsnorm(y, ng_ref[...])


def _ssd_update(state, layer, dtxt, dat, bm, cm, xd, zg, ng, *, tb=8):
    ns = state.shape[1]
    full = lambda *shape: pl.BlockSpec(shape, lambda i: (0,) * len(shape))
    return pl.pallas_call(
        functools.partial(_ssd_update_body, tb=tb, ns=ns),
        grid=(ns // tb,),
        in_specs=[
            pl.BlockSpec((None, tb, H_B, HEAD_DIM, N_B), lambda i: (layer, i, 0, 0, 0)),
            full(D_B, ns), full(D_B, ns), full(ns, G_B * N_B), full(ns, G_B * N_B),
            full(ns, D_B), full(ns, D_B), full(1, D_B),
        ],
        out_specs=[
            pl.BlockSpec((tb, H_B, HEAD_DIM, N_B), lambda i: (i, 0, 0, 0)),
            full(ns, D_B),
        ],
        out_shape=[
            jax.ShapeDtypeStruct((ns, H_B, HEAD_DIM, N_B), F32),
            jax.ShapeDtypeStruct((ns, D_B), F32),
        ],
        scratch_shapes=[pltpu.VMEM((D_B, ns), F32)],
        compiler_params=_cparams("arbitrary"),
        name="ssd_update",
    )(state, dtxt, dat, bm, cm, xd, zg, ng)


QKV_ROWS = 3 * D_C + 2 * SUBLANES


def _paged_body(pt_ref, qkv_ref, *refs, pp, ns):
    k_refs = refs[0:pp]
    v_refs = refs[pp:2 * pp]
    lf_refs = refs[2 * pp:3 * pp]
    ot_ref, qb_ref, m_ref, l_ref, acc_ref, carry_ref = refs[3 * pp:]
    b = pl.program_id(0)
    g = pl.program_id(1)
    lane3 = _iota((H_C, HEAD_DIM, PAGE_SIZE), 2)

    @pl.when(g == 0)
    def _():
        big = _lane_bcast_column(_split3(qkv_ref[...]), b, ns)
        qb = big[0:D_C].reshape(H_C, HEAD_DIM, PAGE_SIZE)
        kb = big[D_C:2 * D_C].reshape(H_C, HEAD_DIM, PAGE_SIZE)
        vb = big[2 * D_C:3 * D_C].reshape(H_C, HEAD_DIM, PAGE_SIZE)
        qb_ref[...] = qb
        m_ref[...] = jnp.sum(qb * kb, axis=1)
        l_ref[...] = jnp.ones((H_C, PAGE_SIZE), F32)
        acc_ref[...] = jnp.where(lane3 == 0, vb, 0.0)
        carry_ref[...] = big[3 * D_C:3 * D_C + H_C]

    @pl.when((b == 0) & (g == 0))
    def _():
        ot_ref[...] = jnp.zeros((D_C, ns), F32)

    later = jnp.where(_iota((PAGE_SIZE, PAGE_SIZE), 0) > _iota((PAGE_SIZE, PAGE_SIZE), 1), 1.0, 0.0).astype(BF16)
    ones = jnp.ones((PAGE_SIZE, PAGE_SIZE), BF16)
    qb = qb_ref[...]
    m = m_ref[...]
    l = l_ref[...]
    acc = acc_ref[...]
    carry = carry_ref[...]
    for t in reversed(range(pp)):
        kt = k_refs[t][0, 0]
        vt = v_refs[t][0, 0]
        lf = lf_refs[t][0, :, 0, 0, :]
        s = jnp.sum(qb * kt, axis=1) + carry + _dot_sel_r(lf, later)
        carry = carry + _dot_sel_r(lf, ones)
        m_new = jnp.maximum(m, jnp.max(s, axis=1, keepdims=True))
        alpha = jnp.exp(m - m_new)
        p = jnp.exp(s - m_new)
        l = alpha * l + jnp.sum(p, axis=1, keepdims=True)
        acc = alpha[:, None, :] * acc + p[:, None, :] * vt
        m = m_new
    m_ref[...] = m
    l_ref[...] = l
    acc_ref[...] = acc
    carry_ref[...] = carry

    @pl.when(g == pl.num_programs(1) - 1)
    def _():
        o = jnp.sum(acc, axis=2, keepdims=True) / l[:, None, :]
        o = o.reshape(D_C, PAGE_SIZE)
        ot_ref[...] = jnp.where(_iota((D_C, ns), 1) == b, o, ot_ref[...])


def _paged_attn(pt_flat, qkv_t, kt_pool, vt_pool, lf_pool, layer, ns, n_pages, *, pp=4):
    ng = n_pages // pp

    def page_map(t):
        return lambda b, g, pt: (layer, pt[b * n_pages + (ng - 1 - g) * pp + t], 0, 0, 0)

    def lf_map(t):
        return lambda b, g, pt: (layer, 0, pt[b * n_pages + (ng - 1 - g) * pp + t], 0, 0)

    kv_specs = [pl.BlockSpec((1, 1, H_C, HEAD_DIM, PAGE_SIZE), page_map(t)) for t in range(pp)]
    lf_specs = [pl.BlockSpec((1, H_C, 1, 1, PAGE_SIZE), lf_map(t)) for t in range(pp)]
    grid_spec = pltpu.PrefetchScalarGridSpec(
        num_scalar_prefetch=1,
        grid=(ns, ng),
        in_specs=[pl.BlockSpec((QKV_ROWS, ns), lambda b, g, pt: (0, 0))] + kv_specs + kv_specs + lf_specs,
        out_specs=pl.BlockSpec((D_C, ns), lambda b, g, pt: (0, 0)),
        scratch_shapes=[
            pltpu.VMEM((H_C, HEAD_DIM, PAGE_SIZE), F32),
            pltpu.VMEM((H_C, PAGE_SIZE), F32),
            pltpu.VMEM((H_C, PAGE_SIZE), F32),
            pltpu.VMEM((H_C, HEAD_DIM, PAGE_SIZE), F32),
            pltpu.VMEM((H_C, PAGE_SIZE), F32),
        ],
    )
    return pl.pallas_call(
        functools.partial(_paged_body, pp=pp, ns=ns),
        grid_spec=grid_spec,
        out_shape=jax.ShapeDtypeStruct((D_C, ns), F32),
        compiler_params=_cparams("arbitrary", "arbitrary"),
        name="paged_attn",
    )(pt_flat, qkv_t, *([kt_pool] * pp), *([vt_pool] * pp), *([lf_pool] * pp))


def _route(lg):
    lane = _iota(lg.shape, 1)
    big = jnp.int32(1 << 20)
    gl = jnp.where(lane < N_GROUPS, lg, NEG_BIG)
    gmax = jnp.max(gl, axis=1, keepdims=True)
    g_sel = jnp.min(jnp.where(gl == gmax, lane, big), axis=1, keepdims=True)
    g_w = 1.0 / jnp.sum(jnp.where(lane < N_GROUPS, jnp.exp(gl - gmax), 0.0), axis=1, keepdims=True)
    in_group = (lane >= N_GROUPS) & (lane < N_GROUPS + N_EXPERTS) & \
        (jnp.right_shift(lane - N_GROUPS, 2) == g_sel)
    el = jnp.where(in_group, lg, NEG_BIG)
    e1 = jnp.max(el, axis=1, keepdims=True)
    i1 = jnp.min(jnp.where(el == e1, lane, big), axis=1, keepdims=True)
    el2 = jnp.where(lane == i1, NEG_BIG, el)
    e2 = jnp.max(el2, axis=1, keepdims=True)
    i2 = jnp.min(jnp.where(el2 == e2, lane, big), axis=1, keepdims=True)
    r = jnp.exp(e2 - e1)
    w1 = g_w / (1.0 + r)
    w2 = g_w * r / (1.0 + r)
    out = jnp.where(lane == 0, (i1 - N_GROUPS).astype(F32), 0.0)
    out = jnp.where(lane == 1, (i2 - N_GROUPS).astype(F32), out)
    out = jnp.where(lane == 2, w1, out)
    out = jnp.where(lane == 3, w2, out)
    return out


def _out_proj_body(x_ref, ya_ref, yb_ref, yc_ref, wah_ref, wbh_ref, wch_ref, wal_ref, wbl_ref, wcl_ref,
                   g_ref, wr_ref, rb_ref, x1_ref, h2_ref, route_ref, *, precise_tiles):
    precise = _is_in(pl.program_id(0), precise_tiles)
    ys = (ya_ref, yb_ref, yc_ref)
    whs = (wah_ref, wbh_ref, wch_ref)
    wls = (wal_ref, wbl_ref, wcl_ref)

    @pl.when(jnp.logical_not(precise))
    def _():
        d = _dot(ys[0][...].astype(BF16), whs[0][...])
        for y, wh in zip(ys[1:], whs[1:]):
            d = d + _dot(y[...].astype(BF16), wh[...])
        x1_ref[...] = x_ref[...] + d

    @pl.when(precise)
    def _():
        d = jnp.zeros(x1_ref.shape, F32)
        for y, wh, wl in zip(ys, whs, wls):
            yh, yl = _split2(y[...])
            d = d + _dot(yh, wh[...]) + _dot(yl, wh[...]) + _dot(yh, wl[...])
        x1_ref[...] = x_ref[...] + d

    x1 = x1_ref[...]
    ms = jnp.mean(x1 * x1, axis=-1, keepdims=True)
    h2 = x1 * lax.rsqrt(ms + EPS) * g_ref[...]
    h2_ref[...] = h2.astype(BF16)
    route_ref[...] = _route(_dg3(h2, wr_ref[...]) + rb_ref[...])


def _out_proj(x, ya, yb, yc, wo_hi, wo_lo, g, wr, rb, precise_tiles, *, tm):
    t = x.shape[0]
    row = lambda w: pl.BlockSpec((tm, w), lambda i: (i, 0))
    const = lambda r, c: _const_spec((r, c), 1)
    parts = lambda w: (w[0:D_A], w[D_A:D_A + D_B], w[D_A + D_B:])
    wspecs = [const(D_A, D_MODEL), const(D_B, D_MODEL), const(D_C, D_MODEL)]
    return pl.pallas_call(
        functools.partial(_out_proj_body, precise_tiles=precise_tiles),
        grid=(t // tm,),
        in_specs=[row(D_MODEL), row(D_A), row(D_B), row(D_C)] + wspecs + wspecs
        + [const(1, D_MODEL), const(D_MODEL, LANES), const(1, LANES)],
        out_specs=[row(D_MODEL), row(D_MODEL), row(LANES)],
        out_shape=[
            jax.ShapeDtypeStruct((t, D_MODEL), F32),
            jax.ShapeDtypeStruct((t, D_MODEL), BF16),
            jax.ShapeDtypeStruct((t, LANES), F32),
        ],
        compiler_params=_cparams("parallel"),
        name="out_proj",
    )(x, ya, yb, yc, *parts(wo_hi), *parts(wo_lo), g, wr, rb)


MOE_TM = 256


def _moe_body(te_ref, nt_ref, xs_ref, wg_ref, wu_ref, wd_ref, ys_ref, wg16, wu16, wd16):
    i = pl.program_id(0)
    changed = te_ref[i] != te_ref[jnp.maximum(i - 1, 0)]

    @pl.when((i == 0) | changed)
    def _():
        wg16[...] = wg_ref[0].astype(BF16)
        wu16[...] = wu_ref[0].astype(BF16)
        wd16[...] = wd_ref[0].astype(BF16)

    @pl.when(i < nt_ref[0])
    def _():
        x = xs_ref[...]
        act = _silu(_dot(x, wg16[...])) * _dot(x, wu16[...])
        ys_ref[...] = _dot(act.astype(BF16), wd16[...])

    @pl.when(i >= nt_ref[0])
    def _():
        ys_ref[...] = jnp.zeros(ys_ref.shape, F32)


def _moe_gemm(tile_expert, n_tiles, xs, wg, wu, wd, layer):
    p_rows = xs.shape[0]
    tm = MOE_TM
    grid_spec = pltpu.PrefetchScalarGridSpec(
        num_scalar_prefetch=2,
        grid=(p_rows // tm,),
        in_specs=[
            pl.BlockSpec((tm, D_MODEL), lambda i, te, nt: (i, 0)),
            pl.BlockSpec((None, 1, D_MODEL, D_FF), lambda i, te, nt: (layer, te[i], 0, 0)),
            pl.BlockSpec((None, 1, D_MODEL, D_FF), lambda i, te, nt: (layer, te[i], 0, 0)),
            pl.BlockSpec((None, 1, D_FF, D_MODEL), lambda i, te, nt: (layer, te[i], 0, 0)),
        ],
        out_specs=pl.BlockSpec((tm, D_MODEL), lambda i, te, nt: (i, 0)),
        scratch_shapes=[
            pltpu.VMEM((D_MODEL, D_FF), BF16),
            pltpu.VMEM((D_MODEL, D_FF), BF16),
            pltpu.VMEM((D_FF, D_MODEL), BF16),
        ],
    )
    return pl.pallas_call(
        _moe_body,
        grid_spec=grid_spec,
        out_shape=jax.ShapeDtypeStruct((p_rows, D_MODEL), F32),
        compiler_params=_cparams("arbitrary"),
        name="moe_gemm",
    )(tile_expert, n_tiles, xs, wg, wu, wd)


def _moe_plan(route, tm):
    t = route.shape[0]
    n_assign = 2 * t
    i32 = jnp.int32
    ids = route[:, 0:2].astype(i32).reshape(-1)
    experts = jnp.arange(N_EXPERTS, dtype=i32)[None, :]
    onehot = ids[:, None] == experts
    csum = jnp.cumsum(onehot.astype(i32), axis=0)
    counts = csum[-1]
    padded = ((counts + tm - 1) // tm) * tm
    pend = jnp.cumsum(padded)
    pstart = pend - padded
    start = jnp.cumsum(counts) - counts
    pick = lambda table, hot: jnp.sum(jnp.where(hot, table[None, :], 0), axis=1)
    pos = pick(pstart, onehot) + jnp.sum(jnp.where(onehot, csum, 0), axis=1) - 1
    p_rows = ((n_assign + N_EXPERTS * tm + tm - 1) // tm) * tm
    p = jnp.arange(p_rows, dtype=i32)
    e_p = jnp.minimum(jnp.sum(p[:, None] >= pend[None, :], axis=1), N_EXPERTS - 1).astype(i32)
    hot_p = e_p[:, None] == experts
    off = p - pick(pstart, hot_p)
    order = jnp.argsort(ids, stable=True).astype(i32)
    src = order[jnp.clip(pick(start, hot_p) + off, 0, n_assign - 1)]
    src_tok = jnp.where(off < pick(counts, hot_p), src // 2, 0)
    n_tiles = (pend[-1] // tm).astype(i32)
    tile_e = e_p[::tm]
    tile_e = jnp.minimum(tile_e, tile_e[jnp.maximum(n_tiles - 1, 0)])
    return src_tok, pos.reshape(t, 2), tile_e, n_tiles.reshape(1)


def _moe_dense3_body(x1_ref, g_ref, route_ref, wg_ref, wu_ref, wd_ref, o_ref):
    e = pl.program_id(0)

    @pl.when(e == 0)
    def _():
        o_ref[...] = jnp.zeros(o_ref.shape, F32)

    x1 = x1_ref[...]
    ms = jnp.mean(x1 * x1, axis=-1, keepdims=True)
    h2 = x1 * lax.rsqrt(ms + EPS) * g_ref[...]
    route = route_ref[...]
    ef = e.astype(F32)
    gate = jnp.where(route[:, 0:1] == ef, route[:, 2:3], 0.0) + jnp.where(route[:, 1:2] == ef, route[:, 3:4], 0.0)
    act = _silu(_dg3(h2, wg_ref[0])) * _dg3(h2, wu_ref[0])
    o_ref[...] += gate * _dg3(act, wd_ref[0])


def _moe_dense3(x1, row0, ns, g, route, wg, wu, wd, layer):
    rows = pl.BlockSpec((ns, D_MODEL), lambda e: (row0 // ns, 0))
    return pl.pallas_call(
        _moe_dense3_body,
        grid=(N_EXPERTS,),
        in_specs=[
            rows,
            pl.BlockSpec((1, D_MODEL), lambda e: (0, 0)),
            pl.BlockSpec((ns, LANES), lambda e: (row0 // ns, 0)),
            pl.BlockSpec((None, 1, D_MODEL, D_FF), lambda e: (layer, e, 0, 0)),
            pl.BlockSpec((None, 1, D_MODEL, D_FF), lambda e: (layer, e, 0, 0)),
            pl.BlockSpec((None, 1, D_FF, D_MODEL), lambda e: (layer, e, 0, 0)),
        ],
        out_specs=pl.BlockSpec((ns, D_MODEL), lambda e: (0, 0)),
        out_shape=jax.ShapeDtypeStruct((ns, D_MODEL), F32),
        compiler_params=_cparams("arbitrary"),
        name="moe_dense3",
    )(x1, g, route, wg, wu, wd)


def _pad_lanes(v, n=LANES):
    return jnp.pad(v, (0, n - v.shape[0]))[None, :]


def _block_diag(w):
    h, d, _ = w.shape
    eye = jnp.eye(h, dtype=w.dtype)
    return (eye[:, None, :, None] * w[:, :, None, :]).reshape(h * d, h * d)


def _hi_lo_body(w_ref, hi_ref, lo_ref):
    hi_ref[...], lo_ref[...] = _split2(w_ref[...])


def _hi_lo(w, *, tr=256):
    rows, cols = w.shape
    spec = pl.BlockSpec((tr, cols), lambda i: (i, 0))
    out = jax.ShapeDtypeStruct((rows, cols), BF16)
    return pl.pallas_call(
        _hi_lo_body,
        grid=(rows // tr,),
        in_specs=[spec],
        out_specs=[spec, spec],
        out_shape=[out, out],
        compiler_params=_cparams("parallel"),
        name="hi_lo_split",
    )(w)


def _pack_w_in(w):
    parts, start = [], 0
    for size in PROJ_SPLITS:
        parts.append(w[:, start:start + size])
        start += size
    xa, ga, z, xbc, dt, q, k, v, f = parts
    padc = lambda a: jnp.pad(a, ((0, 0), (0, LANES - a.shape[1])))
    return _hi_lo(jnp.concatenate([z, xbc, q, k, v, padc(dt), padc(f), xa, ga], axis=1))


def _pack_router(rg, rgb, re, reb):
    wr = jnp.concatenate([rg, jnp.transpose(re, (1, 0, 2)).reshape(D_MODEL, N_EXPERTS)], axis=1)
    wr = jnp.pad(wr, ((0, 0), (0, LANES - wr.shape[1])))
    rb = _pad_lanes(jnp.concatenate([rgb, reb.reshape(-1)]))
    return wr, rb


def kernel(x_prompt, x_sample, cache_k, cache_v, cache_logf, state_lru, state_lru_conv, state_ssm,
           state_ssm_conv, page_table, norm_mix, w_in, lru_conv_w, lru_conv_b, lru_w_a, lru_b_a, lru_w_x,
           lru_b_x, lru_lambda, ssm_conv_w, ssm_conv_b, ssm_dt_bias, ssm_a_log, ssm_d, ssm_norm,
           attn_q_norm, attn_k_norm, attn_f_bias, w_out, norm_ffn, router_group, router_group_bias,
           router_expert, router_expert_bias, moe_w_gate, moe_w_up, moe_w_down):
    nb, seq, _ = x_prompt.shape
    ns = x_sample.shape[0]
    depth = w_in.shape[0]
    n_pages = page_table.shape[1]
    tp = nb * seq
    tm_in, tm_out = 640, 320
    assert (tp + ns) % tm_in == 0 and (tp + ns) % tm_out == 0 and tp % ns == 0
    precise_in = _precise_tiles(tm_in, nb, seq, ns)
    precise_out = _precise_tiles(tm_out, nb, seq, ns)

    x = jnp.concatenate([x_prompt.reshape(tp, D_MODEL), x_sample.reshape(ns, D_MODEL)], axis=0)
    kt_pool = jnp.transpose(cache_k, (0, 1, 3, 4, 2))
    vt_pool = jnp.transpose(cache_v, (0, 1, 3, 4, 2))
    lf_pool = jnp.transpose(cache_logf, (0, 3, 1, 2))[:, :, :, None, :]
    conv_lru = jnp.transpose(state_lru_conv, (0, 2, 1, 3))
    conv_ssm = jnp.transpose(state_ssm_conv, (0, 2, 1, 3))
    pt_flat = page_table.reshape(-1).astype(jnp.int32)

    outs_p, outs_s = [], []
    for l in range(depth):
        precise_tail = l < depth - 1
        wi_hi, wi_lo = _pack_w_in(w_in[l])
        u = _in_proj(x, norm_mix[l][None], wi_hi, wi_lo, precise_in, tm=tm_in)

        lcb, ba, bx, lam = lru_conv_b[l][None], lru_b_a[l][None], lru_b_x[l][None], lru_lambda[l][None]
        wa, wx = _block_diag(lru_w_a[l]), _block_diag(lru_w_x[l])
        scb = ssm_conv_b[l][None]
        dtb, alog = _pad_lanes(ssm_dt_bias[l]), _pad_lanes(ssm_a_log[l])
        ng = ssm_norm[l][None]
        gq = jnp.tile(attn_q_norm[l], H_C)[None]
        gk = jnp.tile(attn_k_norm[l], H_C)[None]
        fb = _pad_lanes(attn_f_bias[l])

        ya_p, lru_h_p, lru_conv_p = _lru_prompt(u, nb, seq, lru_conv_w[l], lcb, wa, ba, wx, bx, lam)
        yb_p, ssm_h_p, ssm_conv_p = _ssd_prompt(u, nb, seq, ssm_conv_w[l], scb, dtb, alog, _pad_lanes(ssm_d[l]), ng,
                                                precise_tail=precise_tail)
        k_p, v_p, lf_p, *attn_ops = _qk_prep_prompt(u, nb, seq, gq, gk, fb, with_lo=precise_tail)
        yc_p = _fox_prompt(attn_ops, nb, seq, precise_tail=precise_tail)

        (ya_s, lru_h_s, lru_conv_s, ssm_conv_s, dtxt, dat, bm, cm, xd, zg) = _sample_pre(
            u, tp, ns, state_lru[l], conv_lru[l], conv_ssm[l], lru_conv_w[l], lcb, wa, ba, wx, bx, lam,
            ssm_conv_w[l], scb, dtb, alog, jnp.repeat(ssm_d[l], HEAD_DIM)[None])
        ssm_h_s, yb_s = _ssd_update(state_ssm, l, dtxt, dat, bm, cm, xd, zg, ng)
        q_s, k_s, lf_s = _qk_prep_sample(u, tp, ns, gq, gk, fb)
        v_s = u[tp:, OFF_V:OFF_V + D_C]
        qkv_t = jnp.concatenate([q_s.T, k_s.T, v_s.T, lf_s[:, 0:2 * SUBLANES].T], axis=0)
        yc_s = _paged_attn(pt_flat, qkv_t, kt_pool, vt_pool, lf_pool, l, ns, n_pages).T

        ya = jnp.concatenate([ya_p, ya_s], axis=0)
        yb = jnp.concatenate([yb_p, yb_s], axis=0)
        yc = jnp.concatenate([yc_p, yc_s], axis=0)
        wo_hi, wo_lo = _hi_lo(w_out[l])
        wr, rb = _pack_router(router_group[l], router_group_bias[l], router_expert[l], router_expert_bias[l])
        x1, h2, route = _out_proj(x, ya, yb, yc, wo_hi, wo_lo, norm_ffn[l][None], wr, rb, precise_out, tm=tm_out)

        src_tok, pos, tile_e, n_tiles = _moe_plan(route, MOE_TM)
        ys = _moe_gemm(tile_e, n_tiles, h2[src_tok], moe_w_gate, moe_w_up, moe_w_down, l)
        x = x1 + route[:, 2:3] * ys[pos[:, 0]] + route[:, 3:4] * ys[pos[:, 1]]
        if l < depth - 1:
            moe_s = _moe_dense3(x1, tp, ns, norm_ffn[l][None], route, moe_w_gate, moe_w_up, moe_w_down, l)
            x = jnp.concatenate([x[:tp], x1[tp:] + moe_s], axis=0)

        outs_p.append((
            k_p.reshape(nb, seq, H_C, HEAD_DIM), v_p.reshape(nb, seq, H_C, HEAD_DIM),
            lf_p[:, 0:H_C].reshape(nb, seq, H_C), lru_h_p[:, 0], lru_conv_p, ssm_h_p, ssm_conv_p))
        outs_s.append((
            k_s.reshape(ns, 1, H_C, HEAD_DIM), v_s.reshape(ns, 1, H_C, HEAD_DIM),
            lf_s[:, 0:H_C].reshape(ns, 1, H_C), lru_h_s, jnp.transpose(lru_conv_s, (1, 0, 2)),
            ssm_h_s, jnp.transpose(ssm_conv_s, (1, 0, 2))))

    stack = lambda states, j: jnp.stack([s[j] for s in states], axis=0)
    return (x[:tp].reshape(nb, seq, D_MODEL), x[tp:].reshape(ns, 1, D_MODEL),
            *[stack(outs_p, j) for j in range(7)], *[stack(outs_s, j) for j in range(7)])
```

```python
import functools
import math

import jax
import jax.numpy as jnp
from jax import lax
from jax.experimental import pallas as pl
from jax.experimental.pallas import tpu as pltpu

F32 = jnp.float32
BF16 = jnp.bfloat16

D_MODEL = 2048
HEAD_DIM = 64
D_A = 512
H_A = 8
D_B = 768
H_B = 12
G_B = 4
N_B = 128
CONV_B = D_B + 2 * G_B * N_B
D_C = 768
H_C = 12
CONV_W = 4
LRU_C = 8.0
SSD_CHUNK = 128
PAGE_SIZE = 128
ATTN_SCALE = 1.0 / math.sqrt(HEAD_DIM)
PROJ_SPLITS = (D_A, D_A, D_B, CONV_B, H_B, D_C, D_C, D_C, H_C)
N_GROUPS = 4
E_PER_GROUP = 4
N_EXPERTS = 16
D_FF = 512
EPS = 1e-6
NEG_BIG = -1e30

LANES = 128
SUBLANES = 8
VMEM_LIMIT = 56 * 1024 * 1024

TAIL = 256
LOG2E = math.log2(math.e)

OFF_Z = 0
OFF_XBC = 768
OFF_Q = 2560
OFF_K = 3328
OFF_V = 4096
OFF_DT = 4864
OFF_F = 4992
OFF_XA = 5120
OFF_GA = 5632
N_U = 6144


def _cparams(*sem):
    return pltpu.CompilerParams(dimension_semantics=sem, vmem_limit_bytes=VMEM_LIMIT)


_NN = (((1,), (0,)), ((), ()))
_NT = (((1,), (1,)), ((), ()))
_TN = (((0,), (0,)), ((), ()))


def _dg(a, b, dims=_NN):
    return lax.dot_general(a, b, dims, preferred_element_type=F32)


def _dot(a, b):
    return _dg(a, b, _NN)


def _split2(x):
    hi = x.astype(BF16)
    lo = (x - hi.astype(F32)).astype(BF16)
    return hi, lo


def _split3(x):
    hi = x.astype(BF16)
    r = x - hi.astype(F32)
    mid = r.astype(BF16)
    lo = (r - mid.astype(F32)).astype(BF16)
    return hi, mid, lo


def _dot_sel_r(x, sel):
    hi, mid, lo = _split3(x)
    return _dot(hi, sel) + _dot(mid, sel) + _dot(lo, sel)


def _dot_sel_l(sel, x):
    hi, mid, lo = _split3(x)
    return _dot(sel, hi) + _dot(sel, mid) + _dot(sel, lo)


def _dg3(a, b, dims=_NN):
    ah, al = _split2(a)
    bh, bl = _split2(b)
    return _dg(ah, bh, dims) + _dg(al, bh, dims) + _dg(ah, bl, dims)


def _mm(a, b, dims, precise):
    if precise:
        return _dg3(a, b, dims)
    return _dg(a.astype(BF16), b.astype(BF16), dims)


def _sigmoid(x):
    return 1.0 / (1.0 + jnp.exp(-x))


def _silu(x):
    return x * _sigmoid(x)


def _log_sigmoid(x):
    return jnp.minimum(x, 0.0) - jnp.log1p(jnp.exp(-jnp.abs(x)))


def _softplus(x):
    return jnp.maximum(x, 0.0) + jnp.log1p(jnp.exp(-jnp.abs(x)))


def _gelu_tanh(x):
    return 0.5 * x * (1.0 + jnp.tanh(math.sqrt(2.0 / math.pi) * (x + 0.044715 * (x * x * x))))


def _iota(shape, dim):
    return lax.broadcasted_iota(jnp.int32, shape, dim)


def _seg_matrix(n_rows, n_cols, seg, transpose=False):
    if transpose:
        m = (_iota((n_cols, n_rows), 1) // seg) == _iota((n_cols, n_rows), 0)
    else:
        m = (_iota((n_rows, n_cols), 0) // seg) == _iota((n_rows, n_cols), 1)
    return jnp.where(m, 1.0, 0.0).astype(BF16)


def _precise_tiles(tm, nb, seq, ns):
    spans = [((b + 1) * seq - TAIL, (b + 1) * seq) for b in range(nb)] + [(nb * seq, nb * seq + ns)]
    n_tiles = (nb * seq + ns) // tm
    return tuple(i for i in range(n_tiles) if any(lo < (i + 1) * tm and hi > i * tm for lo, hi in spans))


def _is_in(i, ids):
    hit = i == ids[0]
    for t in ids[1:]:
        hit = hit | (i == t)
    return hit


def _const_spec(shape, n_grid):
    zeros = (0,) * len(shape)
    return pl.BlockSpec(shape, lambda *_: zeros, pipeline_mode=pl.Buffered(1))


def _in_proj_body(x_ref, g_ref, wh_ref, wl_ref, o_ref, xh_ref, xl_ref, *, precise_tiles):
    precise = _is_in(pl.program_id(0), precise_tiles)

    @pl.when(pl.program_id(1) == 0)
    def _():
        x = x_ref[...]
        ms = jnp.mean(x * x, axis=-1, keepdims=True)
        xh_ref[...], xl_ref[...] = _split2(x * lax.rsqrt(ms + EPS) * g_ref[...])

    @pl.when(jnp.logical_not(precise))
    def _():
        o_ref[...] = _dot(xh_ref[...], wh_ref[...])

    @pl.when(precise)
    def _():
        o_ref[...] = (_dot(xh_ref[...], wh_ref[...]) + _dot(xl_ref[...], wh_ref[...])
                      + _dot(xh_ref[...], wl_ref[...]))


def _in_proj(x, g, wh, wl, precise_tiles, *, tm, tn=1024):
    t = x.shape[0]
    return pl.pallas_call(
        functools.partial(_in_proj_body, precise_tiles=precise_tiles),
        grid=(t // tm, N_U // tn),
        in_specs=[
            pl.BlockSpec((tm, D_MODEL), lambda i, j: (i, 0)),
            pl.BlockSpec((1, D_MODEL), lambda i, j: (0, 0)),
            pl.BlockSpec((D_MODEL, tn), lambda i, j: (0, j)),
            pl.BlockSpec((D_MODEL, tn), lambda i, j: (0, jnp.where(_is_in(i, precise_tiles), j, 0))),
        ],
        out_specs=pl.BlockSpec((tm, tn), lambda i, j: (i, j)),
        out_shape=jax.ShapeDtypeStruct((t, N_U), F32),
        scratch_shapes=[pltpu.VMEM((tm, D_MODEL), BF16), pltpu.VMEM((tm, D_MODEL), BF16)],
        compiler_params=_cparams("parallel", "arbitrary"),
        name="in_proj",
    )(x, g, wh, wl)


def _lru_gates(xc, wa, ba, wx, bx, lam):
    r = _sigmoid(_dg3(xc, wa) + ba)
    i = _sigmoid(_dg3(xc, wx) + bx)
    log_a = LRU_C * r * _log_sigmoid(lam)
    a = jnp.exp(log_a)
    th = jnp.tanh(log_a)
    mult = jnp.sqrt(-2.0 * th / (1.0 - th))
    return a, mult * (i * xc)


def _lru_prompt_body(u_ref, cw_ref, cb_ref, wa_ref, ba_ref, wx_ref, bx_ref, lam_ref,
                     y_ref, h_out_ref, conv_out_ref, xp_ref, h_ref, *, tl):
    i = pl.program_id(1)

    @pl.when(i == 0)
    def _():
        xp_ref[0:SUBLANES, :] = jnp.zeros((SUBLANES, D_A), F32)
        h_ref[...] = jnp.zeros((1, D_A), F32)

    x = u_ref[:, 0:D_A]
    ga = u_ref[:, D_A:2 * D_A]
    xp_ref[SUBLANES:SUBLANES + tl, :] = x
    cw = cw_ref[...]
    xc = cb_ref[...] + cw[3:4, :] * x
    for j in range(CONV_W - 1):
        xc = xc + cw[j:j + 1, :] * xp_ref[pl.ds(SUBLANES - 3 + j, tl), :]
    xp_ref[0:SUBLANES, :] = x[tl - SUBLANES:tl, :]

    a, u = _lru_gates(xc, wa_ref[...], ba_ref[...], wx_ref[...], bx_ref[...], lam_ref[...])

    row = _iota((tl, D_A), 0)
    s = 1
    while s < tl:
        keep = row >= s
        a_sh = jnp.where(keep, pltpu.roll(a, s, axis=0), 1.0)
        u_sh = jnp.where(keep, pltpu.roll(u, s, axis=0), 0.0)
        u = u + a * u_sh
        a = a * a_sh
        s *= 2
    h = a * h_ref[...] + u
    h_ref[...] = h[tl - 1:tl, :]
    y_ref[...] = h * _gelu_tanh(ga)

    @pl.when(i == pl.num_programs(1) - 1)
    def _():
        h_out_ref[0] = h[tl - 1:tl, :]
        conv_out_ref[0] = x[tl - 3:tl, :]


def _lru_prompt(u, nb, seq, cw, cb, wa, ba, wx, bx, lam, *, tl=512):
    nt = seq // tl
    vec = lambda n: pl.BlockSpec((1, n), lambda b, i: (0, 0))
    mat = lambda r, c: pl.BlockSpec((r, c), lambda b, i: (0, 0))
    rb = lambda b, i: b * nt + i
    return pl.pallas_call(
        functools.partial(_lru_prompt_body, tl=tl),
        grid=(nb, nt),
        in_specs=[
            pl.BlockSpec((tl, 2 * D_A), lambda b, i: (rb(b, i), OFF_XA // (2 * D_A))),
            mat(CONV_W, D_A), vec(D_A), mat(D_A, D_A), vec(D_A), mat(D_A, D_A), vec(D_A), vec(D_A),
        ],
        out_specs=[
            pl.BlockSpec((tl, D_A), lambda b, i: (rb(b, i), 0)),
            pl.BlockSpec((1, 1, D_A), lambda b, i: (b, 0, 0)),
            pl.BlockSpec((1, CONV_W - 1, D_A), lambda b, i: (b, 0, 0)),
        ],
        out_shape=[
            jax.ShapeDtypeStruct((u.shape[0], D_A), F32),
            jax.ShapeDtypeStruct((nb, 1, D_A), F32),
            jax.ShapeDtypeStruct((nb, CONV_W - 1, D_A), F32),
        ],
        scratch_shapes=[pltpu.VMEM((SUBLANES + tl, D_A), F32), pltpu.VMEM((1, D_A), F32)],
        compiler_params=_cparams("parallel", "arbitrary"),
        name="lru_prompt",
    )(u, cw, cb, wa, ba, wx, bx, lam)


def _group_rmsnorm(y, g):
    gw = D_B // G_B
    ss = _dot_sel_r(y * y, _seg_matrix(D_B, LANES, gw))
    rs = lax.rsqrt(ss * (1.0 / gw) + EPS)
    return y * _dot_sel_r(rs, _seg_matrix(D_B, LANES, gw, transpose=True)) * g


def _ssd_chunk(x, bm, cm, dt, dta, dsk, h_ref, precise):
    q = SSD_CHUNK
    tri = jnp.where(_iota((q, q), 0) >= _iota((q, q), 1), 1.0, 0.0).astype(BF16)
    cum = _dot_sel_l(tri, dta)
    cum_t = cum.T
    dt_t = dt.T
    causal = _iota((q, q), 0) >= _iota((q, q), 1)
    ys = []
    for g in range(G_B):
        bg = bm[:, g * N_B:(g + 1) * N_B]
        cg = cm[:, g * N_B:(g + 1) * N_B]
        scores = _mm(cg, bg, _NT, precise)
        for r in range(H_B // G_B):
            h = g * (H_B // G_B) + r
            xh = x[:, h * HEAD_DIM:(h + 1) * HEAD_DIM]
            cl = cum[:, h:h + 1]
            cs = cum_t[h:h + 1, :]
            c_last = cum_t[h:h + 1, q - 1:q]
            decay = jnp.exp(jnp.where(causal, cl - cs, NEG_BIG))
            w = scores * decay * dt_t[h:h + 1, :]
            y_diag = _mm(w, xh, _NN, precise)
            h_prev = h_ref[h]
            y_off = _mm(cg, h_prev, _NT, precise) * jnp.exp(cl)
            de = jnp.exp(c_last - cl) * dt[:, h:h + 1]
            st = _mm(xh * de, bg, _TN, precise)
            h_ref[h] = jnp.exp(c_last) * h_prev + st
            ys.append(y_diag + y_off + dsk[:, h:h + 1] * xh)
    return jnp.concatenate(ys, axis=1)


def _ssd_prompt_body(u_ref, s_ref, cw_ref, cb_ref, dtb_ref, alog_ref, dsk_ref, ng_ref,
                     y_ref, h_out_ref, conv_out_ref, xp_ref, h_ref, *, precise_tail):
    c = pl.program_id(1)
    nc = pl.num_programs(1)
    q = SSD_CHUNK

    @pl.when(c == 0)
    def _():
        xp_ref[0:SUBLANES, :] = jnp.zeros((SUBLANES, CONV_B), F32)
        h_ref[...] = jnp.zeros((H_B, HEAD_DIM, N_B), F32)

    z = u_ref[:, OFF_Z:OFF_Z + D_B]
    xbc = u_ref[:, OFF_XBC:OFF_XBC + CONV_B]
    xp_ref[SUBLANES:SUBLANES + q, :] = xbc
    cw = cw_ref[...]
    xc = cb_ref[...] + cw[3:4, :] * xbc
    for j in range(CONV_W - 1):
        xc = xc + cw[j:j + 1, :] * xp_ref[pl.ds(SUBLANES - 3 + j, q), :]
    xp_ref[0:SUBLANES, :] = xbc[q - SUBLANES:q, :]
    xc = _silu(xc)
    x = xc[:, 0:D_B]
    bm = xc[:, D_B:D_B + G_B * N_B]
    cm = xc[:, D_B + G_B * N_B:]
    dt = _softplus(s_ref[:, 0:LANES] + dtb_ref[...])
    dta = dt * (-jnp.exp(alog_ref[...]))
    gate = _silu(z)

    def finish(precise):
        y = _ssd_chunk(x, bm, cm, dt, dta, dsk_ref[...], h_ref, precise) * gate
        y_ref[...] = _group_rmsnorm(y, ng_ref[...])

    if precise_tail:
        in_tail = c >= nc - TAIL // q
        pl.when(in_tail)(lambda: finish(True))
        pl.when(jnp.logical_not(in_tail))(lambda: finish(False))
    else:
        finish(False)

    @pl.when(c == nc - 1)
    def _():
        h_out_ref[0] = h_ref[...]
        conv_out_ref[0] = xbc[q - 3:q, :]


def _ssd_prompt(u, nb, seq, cw, cb, dtb, alog, dsk, ng, *, precise_tail):
    nc = seq // SSD_CHUNK
    q = SSD_CHUNK
    vec = lambda n: pl.BlockSpec((1, n), lambda b, c: (0, 0))
    rb = lambda b, c: b * nc + c
    return pl.pallas_call(
        functools.partial(_ssd_prompt_body, precise_tail=precise_tail),
        grid=(nb, nc),
        in_specs=[
            pl.BlockSpec((q, OFF_Q), lambda b, c: (rb(b, c), 0)),
            pl.BlockSpec((q, 2 * LANES), lambda b, c: (rb(b, c), OFF_DT // (2 * LANES))),
            pl.BlockSpec((CONV_W, CONV_B), lambda b, c: (0, 0)),
            vec(CONV_B), vec(LANES), vec(LANES), vec(LANES), vec(D_B),
        ],
        out_specs=[
            pl.BlockSpec((q, D_B), lambda b, c: (rb(b, c), 0)),
            pl.BlockSpec((1, H_B, HEAD_DIM, N_B), lambda b, c: (b, 0, 0, 0)),
            pl.BlockSpec((1, CONV_W - 1, CONV_B), lambda b, c: (b, 0, 0)),
        ],
        out_shape=[
            jax.ShapeDtypeStruct((u.shape[0], D_B), F32),
            jax.ShapeDtypeStruct((nb, H_B, HEAD_DIM, N_B), F32),
            jax.ShapeDtypeStruct((nb, CONV_W - 1, CONV_B), F32),
        ],
        scratch_shapes=[pltpu.VMEM((SUBLANES + q, CONV_B), F32),
                        pltpu.VMEM((H_B, HEAD_DIM, N_B), F32)],
        compiler_params=_cparams("parallel", "arbitrary"),
        name="ssd_prompt",
    )(u, u, cw, cb, dtb, alog, dsk, ng)


def _head_rmsnorm(x, g):
    ss = _dot_sel_r(x * x, _seg_matrix(D_C, LANES, HEAD_DIM))
    rs = lax.rsqrt(ss * (1.0 / HEAD_DIM) + EPS)
    return x * _dot_sel_r(rs, _seg_matrix(D_C, LANES, HEAD_DIM, transpose=True)) * g


D_CX = H_C * LANES


def _widen_heads(x, extras):
    parts = []
    for h in range(H_C):
        parts += [x[:, h * HEAD_DIM:(h + 1) * HEAD_DIM], extras[h]]
    return jnp.concatenate(parts, axis=1)


def _qk_prep_prompt_body(u_ref, gq_ref, gk_ref, fb_ref, k_ref, v_ref, lf_ref, qh_ref, kh_ref, vh_ref,
                         *rest, tl, with_lo):
    if with_lo:
        ql_ref, kl_ref, vl_ref, carry_ref = rest
    else:
        (carry_ref,) = rest
    q = u_ref[:, 0:D_C]
    k = u_ref[:, D_C:2 * D_C]
    v = u_ref[:, 2 * D_C:3 * D_C]
    f_raw = u_ref[:, OFF_F - OFF_Q:OFF_F - OFF_Q + LANES]
    qn = _head_rmsnorm(q, gq_ref[...]) * (ATTN_SCALE * LOG2E)
    kn = _head_rmsnorm(k, gk_ref[...])
    k_ref[...] = kn
    v_ref[...] = v
    lf = _log_sigmoid(f_raw + fb_ref[...])
    lf_ref[...] = lf

    @pl.when(pl.program_id(1) == 0)
    def _():
        carry_ref[...] = jnp.zeros((1, LANES), F32)

    tri = jnp.where(_iota((tl, tl), 0) >= _iota((tl, tl), 1), 1.0, 0.0).astype(BF16)
    c = _dot_sel_l(tri, lf) + carry_ref[...]
    carry_ref[...] = c[tl - 1:tl, :]
    b3 = [p.astype(F32) for p in _split3(c * (-LOG2E))]

    lane = _iota((tl, HEAD_DIM), 1)
    zeros = jnp.zeros((tl, HEAD_DIM), F32)
    ones = jnp.ones((tl, HEAD_DIM), F32)
    sel3 = jnp.where(lane < 3, 1.0, 0.0)
    bias = [jnp.where(lane == 0, b3[0][:, h:h + 1],
                      jnp.where(lane == 1, b3[1][:, h:h + 1],
                                jnp.where(lane == 2, b3[2][:, h:h + 1], 0.0))) for h in range(H_C)]

    def hi_lo(x):
        hi = x.astype(BF16)
        return hi, x - hi.astype(F32)

    q_hi, q_rem = hi_lo(qn)
    k_hi, k_rem = hi_lo(kn)
    v_hi, v_rem = hi_lo(v)
    qh_ref[...] = _widen_heads(q_hi.astype(F32), [sel3] * H_C).astype(BF16)
    kh_ref[...] = _widen_heads(k_hi.astype(F32), bias).astype(BF16)
    vh_ref[0] = _widen_heads(v_hi.astype(F32), [ones] * H_C).T.astype(BF16)
    if with_lo:
        ql_ref[...] = _widen_heads(q_rem, [zeros] * H_C).astype(BF16)
        kl_ref[...] = _widen_heads(k_rem, [zeros] * H_C).astype(BF16)
        vl_ref[0] = _widen_heads(v_rem, [zeros] * H_C).T.astype(BF16)


def _qk_prep_prompt(u, nb, seq, gq, gk, fb, *, with_lo, tl=512):
    nt = seq // tl
    rows = nb * seq
    wq = OFF_XA - OFF_Q
    vec = lambda n: pl.BlockSpec((1, n), lambda b, i: (0, 0))
    nat = lambda w: pl.BlockSpec((tl, w), lambda b, i: (b * nt + i, 0))
    sx = jax.ShapeDtypeStruct((rows, D_CX), BF16)
    s32 = jax.ShapeDtypeStruct((rows, D_C), F32)
    vt_spec = pl.BlockSpec((1, D_CX, tl), lambda b, i: (b, 0, i))
    sxt = jax.ShapeDtypeStruct((nb, D_CX, seq), BF16)
    reps = 2 if with_lo else 1
    return pl.pallas_call(
        functools.partial(_qk_prep_prompt_body, tl=tl, with_lo=with_lo),
        grid=(nb, nt),
        in_specs=[
            pl.BlockSpec((tl, wq), lambda b, i: (b * nt + i, OFF_Q // wq)),
            vec(D_C), vec(D_C), vec(LANES),
        ],
        out_specs=[nat(D_C), nat(D_C), nat(LANES)] + [nat(D_CX), nat(D_CX), vt_spec] * reps,
        out_shape=[s32, s32, jax.ShapeDtypeStruct((rows, LANES), F32)] + [sx, sx, sxt] * reps,
        scratch_shapes=[pltpu.VMEM((1, LANES), F32)],
        compiler_params=_cparams("parallel", "arbitrary"),
        name="qk_prep_prompt",
    )(u, gq, gk, fb)


def _qk_prep_sample_body(u_ref, gq_ref, gk_ref, fb_ref, q_ref, k_ref, lf_ref):
    q = u_ref[:, 0:D_C]
    k = u_ref[:, D_C:2 * D_C]
    f_raw = u_ref[:, OFF_F - OFF_Q:OFF_F - OFF_Q + LANES]
    q_ref[...] = _head_rmsnorm(q, gq_ref[...]) * ATTN_SCALE
    k_ref[...] = _head_rmsnorm(k, gk_ref[...])
    lf_ref[...] = _log_sigmoid(f_raw + fb_ref[...])


def _qk_prep_sample(u, row0, ns, gq, gk, fb):
    wq = OFF_XA - OFF_Q
    full = lambda r, c: pl.BlockSpec((r, c), lambda i: (0, 0))
    return pl.pallas_call(
        _qk_prep_sample_body,
        grid=(1,),
        in_specs=[pl.BlockSpec((ns, wq), lambda i: (row0 // ns, OFF_Q // wq)),
                  full(1, D_C), full(1, D_C), full(1, LANES)],
        out_specs=[full(ns, D_C), full(ns, D_C), full(ns, LANES)],
        out_shape=[jax.ShapeDtypeStruct((ns, D_C), F32), jax.ShapeDtypeStruct((ns, D_C), F32),
                   jax.ShapeDtypeStruct((ns, LANES), F32)],
        compiler_params=_cparams("arbitrary"),
        name="qk_prep_sample",
    )(u, gq, gk, fb)


def _fox_tile(i, refs, tq, precise):
    if precise:
        qh_ref, kh_ref, vh_ref, ql_ref, kl_ref, vl_ref, o_ref = refs
    else:
        qh_ref, kh_ref, vh_ref, o_ref = refs
    causal_t = _iota((tq, tq), 0) <= _iota((tq, tq), 1)
    heads = [slice(hh * LANES, (hh + 1) * LANES) for hh in range(2)]
    qh = [qh_ref[:, sl] for sl in heads]
    ql = [ql_ref[:, sl] for sl in heads] if precise else None

    def scores(j, hh):
        off = pl.multiple_of(j * tq, tq)
        kh = kh_ref[pl.ds(off, tq), heads[hh]]
        s = _dg(kh, qh[hh], _NT)
        if precise:
            s = s + _dg(kh, ql[hh], _NT) + _dg(kl_ref[pl.ds(off, tq), heads[hh]], qh[hh], _NT)
        return s

    def update(j, hh, state, s, masked):
        m, acc = state
        off = pl.multiple_of(j * tq, tq)
        vh = vh_ref[0, heads[hh], pl.ds(off, tq)]
        if masked:
            s = jnp.where(causal_t, s, NEG_BIG)
        m_new = jnp.maximum(m, jnp.max(s, axis=0, keepdims=True))
        alpha = jnp.exp2(m - m_new)
        p = jnp.exp2(s - m_new)
        if precise:
            p_hi, p_lo = _split2(p)
            pv = _dot(vh, p_hi) + _dot(vh, p_lo) + _dot(vl_ref[0, heads[hh], pl.ds(off, tq)], p_hi)
        else:
            pv = _dot(vh, p.astype(BF16))
        return m_new, alpha * acc + pv

    def step(j, carry):
        out = []
        for hh in range(2):
            state, s_cur = carry[hh]
            s_next = scores(j + 1, hh)
            out.append((update(j, hh, state, s_cur, False), s_next))
        return tuple(out)

    init = tuple(((jnp.full((1, tq), NEG_BIG, F32), jnp.zeros((LANES, tq), F32)), scores(0, hh))
                 for hh in range(2))
    carry = lax.fori_loop(0, i, step, init)
    carry = tuple(update(i, hh, state, s_cur, True) for hh, (state, s_cur) in enumerate(carry))
    o_t = jnp.concatenate([acc[0:HEAD_DIM] / acc[HEAD_DIM:HEAD_DIM + 1] for _, acc in carry], axis=0)
    o_ref[...] = o_t.T


def _fox_body(*refs, tq, precise_tail):
    i = pl.program_id(2)
    if precise_tail:
        last = i == pl.num_programs(2) - 1
        plain = refs[0:3] + refs[6:]
        pl.when(last)(lambda: _fox_tile(i, refs, tq, True))
        pl.when(jnp.logical_not(last))(lambda: _fox_tile(i, plain, tq, False))
    else:
        _fox_tile(i, refs, tq, False)


def _fox_prompt(operands, nb, seq, total_rows, *, precise_tail, tq=512):
    nq = seq // tq
    npair = H_C // 2
    qspec = pl.BlockSpec((tq, 2 * LANES), lambda b, p, i: (b * nq + i, p))
    kspec = pl.BlockSpec((seq, 2 * LANES), lambda b, p, i: (b, p))
    vspec = pl.BlockSpec((1, 2 * LANES, seq), lambda b, p, i: (b, p, 0))
    return pl.pallas_call(
        functools.partial(_fox_body, tq=tq, precise_tail=precise_tail),
        grid=(nb, npair, nq),
        in_specs=[qspec, kspec, vspec] * (2 if precise_tail else 1),
        out_specs=pl.BlockSpec((tq, LANES), lambda b, p, i: (b * nq + i, p)),
        out_shape=jax.ShapeDtypeStruct((total_rows, D_C), F32),
        compiler_params=_cparams("parallel", "parallel", "arbitrary"),
        name="fox_prompt",
    )(*operands)


def _sample_pre_body(u_ref, hl_ref, cl_ref, cs_ref, lcw_ref, lcb_ref, wa_ref, ba_ref, wx_ref, bx_ref,
                     lam_ref, scw_ref, scb_ref, dtb_ref, alog_ref, dsk_ref, ya_in_ref,
                     ya_ref, hnew_ref, lconv_ref, sconv_ref, dtxt_ref, dat_ref, bm_ref, cm_ref,
                     xd_ref, zg_ref):
    xa = u_ref[:, OFF_XA:OFF_XA + D_A]
    ga = u_ref[:, OFF_GA:OFF_GA + D_A]
    cw = lcw_ref[...]
    xc = lcb_ref[...] + cw[3:4, :] * xa
    for j in range(CONV_W - 1):
        xc = xc + cw[j:j + 1, :] * cl_ref[j]
    lconv_ref[0] = cl_ref[1]
    lconv_ref[1] = cl_ref[2]
    lconv_ref[2] = xa
    a, uu = _lru_gates(xc, wa_ref[...], ba_ref[...], wx_ref[...], bx_ref[...], lam_ref[...])
    h = a * hl_ref[...] + uu
    hnew_ref[...] = h
    ya_ref[...] = h * _gelu_tanh(ga)

    z = u_ref[:, OFF_Z:OFF_Z + D_B]
    xbc = u_ref[:, OFF_XBC:OFF_XBC + CONV_B]
    cw = scw_ref[...]
    xs = scb_ref[...] + cw[3:4, :] * xbc
    for j in range(CONV_W - 1):
        xs = xs + cw[j:j + 1, :] * cs_ref[j]
    sconv_ref[0] = cs_ref[1]
    sconv_ref[1] = cs_ref[2]
    sconv_ref[2] = xbc
    xs = _silu(xs)
    x = xs[:, 0:D_B]
    bm_ref[...] = xs[:, D_B:D_B + G_B * N_B]
    cm_ref[...] = xs[:, D_B + G_B * N_B:]
    dt = _softplus(u_ref[:, OFF_DT:OFF_DT + LANES] + dtb_ref[...])
    da = jnp.exp(dt * (-jnp.exp(alog_ref[...])))
    expand = _seg_matrix(D_B, LANES, HEAD_DIM, transpose=True)
    dtxt_ref[...] = (_dot_sel_r(dt, expand) * x).T
    dat_ref[...] = _dot_sel_r(da, expand).T
    xd_ref[...] = dsk_ref[...] * x
    zg_ref[...] = _silu(z)


def _sample_pre(u, row0, ns, h_lru, conv_lru, conv_ssm, lcw, lcb, wa, ba, wx, bx, lam, scw, scb, dtb, alog, dsk_e,
                ya_all):
    full = lambda *shape: pl.BlockSpec(shape, lambda i: (0,) * len(shape))
    rows_s = lambda w: pl.BlockSpec((ns, w), lambda i: (row0 // ns, 0))
    f = lambda *shape: jax.ShapeDtypeStruct(shape, F32)
    return pl.pallas_call(
        _sample_pre_body,
        grid=(1,),
        in_specs=[
            pl.BlockSpec((ns, N_U), lambda i: (row0 // ns, 0)),
            full(ns, D_A), full(CONV_W - 1, ns, D_A), full(CONV_W - 1, ns, CONV_B),
            full(CONV_W, D_A), full(1, D_A), full(D_A, D_A), full(1, D_A), full(D_A, D_A), full(1, D_A),
            full(1, D_A), full(CONV_W, CONV_B), full(1, CONV_B), full(1, LANES), full(1, LANES), full(1, D_B),
            rows_s(D_A),
        ],
        out_specs=[
            rows_s(D_A), full(ns, D_A), full(CONV_W - 1, ns, D_A), full(CONV_W - 1, ns, CONV_B),
            full(D_B, ns), full(D_B, ns), full(ns, G_B * N_B), full(ns, G_B * N_B), full(ns, D_B), full(ns, D_B),
        ],
        out_shape=[
            f(*ya_all.shape), f(ns, D_A), f(CONV_W - 1, ns, D_A), f(CONV_W - 1, ns, CONV_B),
            f(D_B, ns), f(D_B, ns), f(ns, G_B * N_B), f(ns, G_B * N_B), f(ns, D_B), f(ns, D_B),
        ],
        input_output_aliases={16: 0},
        compiler_params=_cparams("arbitrary"),
        name="sample_pre",
    )(u, h_lru, conv_lru, conv_ssm, lcw, lcb, wa, ba, wx, bx, lam, scw, scb, dtb, alog, dsk_e, ya_all)


def _lane_bcast_column(parts, b, ns):
    onehot = jnp.where(_iota((ns, LANES), 0) == b, 1.0, 0.0).astype(BF16)
    out = _dot(parts[0], onehot)
    for p in parts[1:]:
        out = out + _dot(p, onehot)
    return out


def _ssd_update_body(st_ref, dtxt_ref, dat_ref, bm_ref, cm_ref, xd_ref, zg_ref, ng_ref, y_in_ref,
                     st_out_ref, y_ref, yt_ref, *, tb, ns):
    i = pl.program_id(0)

    @pl.when(i == 0)
    def _():
        yt_ref[...] = jnp.zeros((D_B, ns), F32)

    dtx3 = _split3(dtxt_ref[...])
    da3 = _split3(dat_ref[...])
    lane = _iota((D_B, ns), 1)
    rows_per_group = D_B // G_B

    def group_rows(row):
        return jnp.concatenate(
            [jnp.broadcast_to(row[:, g * N_B:(g + 1) * N_B], (rows_per_group, N_B)) for g in range(G_B)], axis=0)

    for t in range(tb):
        b = i * tb + t
        xb = _lane_bcast_column(dtx3, b, ns)
        dab = _lane_bcast_column(da3, b, ns)
        bexp = group_rows(bm_ref[pl.ds(b, 1), :])
        cexp = group_rows(cm_ref[pl.ds(b, 1), :])
        s_new = dab * st_ref[t].reshape(D_B, N_B) + xb * bexp
        st_out_ref[t] = s_new.reshape(H_B, HEAD_DIM, N_B)
        ycol = jnp.sum(s_new * cexp, axis=1, keepdims=True)
        yt_ref[...] = jnp.where(lane == b, ycol, yt_ref[...])

    @pl.when(i == pl.num_programs(0) - 1)
    def _():
        y = (yt_ref[...].T + xd_ref[...]) * zg_ref[...]
        y_ref[...] = _group_rmsnorm(y, ng_ref[...])


def _ssd_update(state, layer, dtxt, dat, bm, cm, xd, zg, ng, yb_all, row0, *, tb=8):
    ns = state.shape[1]
    full = lambda *shape: pl.BlockSpec(shape, lambda i: (0,) * len(shape))
    rows_s = pl.BlockSpec((ns, D_B), lambda i: (row0 // ns, 0))
    return pl.pallas_call(
        functools.partial(_ssd_update_body, tb=tb, ns=ns),
        grid=(ns // tb,),
        in_specs=[
            pl.BlockSpec((None, tb, H_B, HEAD_DIM, N_B), lambda i: (layer, i, 0, 0, 0)),
            full(D_B, ns), full(D_B, ns), full(ns, G_B * N_B), full(ns, G_B * N_B),
            full(ns, D_B), full(ns, D_B), full(1, D_B), rows_s,
        ],
        out_specs=[
            pl.BlockSpec((tb, H_B, HEAD_DIM, N_B), lambda i: (i, 0, 0, 0)),
            rows_s,
        ],
        out_shape=[
            jax.ShapeDtypeStruct((ns, H_B, HEAD_DIM, N_B), F32),
            jax.ShapeDtypeStruct(yb_all.shape, F32),
        ],
        scratch_shapes=[pltpu.VMEM((D_B, ns), F32)],
        input_output_aliases={8: 1},
        compiler_params=_cparams("arbitrary"),
        name="ssd_update",
    )(state, dtxt, dat, bm, cm, xd, zg, ng, yb_all)


def _logf_suffix_body(lf_ref, suf_ref, tot_ref):
    later = jnp.where(_iota((PAGE_SIZE, PAGE_SIZE), 0) > _iota((PAGE_SIZE, PAGE_SIZE), 1), 1.0, 0.0).astype(BF16)
    lf = lf_ref[...]
    suf_ref[...] = _dot_sel_r(lf, later)
    tot_ref[...] = _dot_sel_r(lf, jnp.ones((PAGE_SIZE, PAGE_SIZE), BF16))


def _logf_suffix(lf_rows, *, tr=2048):
    rows = lf_rows.shape[0]
    spec = pl.BlockSpec((tr, PAGE_SIZE), lambda i: (i, 0))
    out = jax.ShapeDtypeStruct((rows, PAGE_SIZE), F32)
    return pl.pallas_call(
        _logf_suffix_body,
        grid=(rows // tr,),
        in_specs=[spec],
        out_specs=[spec, spec],
        out_shape=[out, out],
        compiler_params=_cparams("parallel"),
        name="logf_suffix",
    )(lf_rows)


ML_ROWS = 4 * SUBLANES


def _paged_body(pt_ref, qt_ref, *refs, pp, ns):
    k_refs = refs[0:pp]
    v_refs = refs[pp:2 * pp]
    suf_refs = refs[2 * pp:3 * pp]
    off_ref, acct_ref, mlt_ref, qb_ref, m_ref, l_ref, acc_ref = refs[3 * pp:]
    b = pl.program_id(0)
    g = pl.program_id(1)

    @pl.when(g == 0)
    def _():
        qb_ref[...] = _lane_bcast_column(_split3(qt_ref[...]), b, ns).reshape(H_C, HEAD_DIM, PAGE_SIZE)
        m_ref[...] = jnp.full((H_C, PAGE_SIZE), NEG_BIG, F32)
        l_ref[...] = jnp.zeros((H_C, PAGE_SIZE), F32)
        acc_ref[...] = jnp.zeros((H_C, HEAD_DIM, PAGE_SIZE), F32)

    @pl.when((b == 0) & (g == 0))
    def _():
        acct_ref[...] = jnp.zeros(acct_ref.shape, F32)
        mlt_ref[...] = jnp.zeros(mlt_ref.shape, F32)

    for h in range(H_C):
        qh = qb_ref[h]
        s = [jnp.sum(qh * k_refs[t][0, 0, h], axis=0, keepdims=True)
             + suf_refs[t][0, h, 0] + off_ref[0, t, h:h + 1, :] for t in range(pp)]
        m_old = m_ref[h:h + 1, :]
        m_new = m_old
        for st in s:
            m_new = jnp.maximum(m_new, jnp.max(st, axis=1, keepdims=True))
        alpha = jnp.exp(m_old - m_new)
        p = [jnp.exp(st - m_new) for st in s]
        l_new = alpha * l_ref[h:h + 1, :]
        acc = alpha * acc_ref[h]
        for t in range(pp):
            l_new = l_new + jnp.sum(p[t], axis=1, keepdims=True)
            acc = acc + p[t] * v_refs[t][0, 0, h]
        m_ref[h:h + 1, :] = m_new
        l_ref[h:h + 1, :] = l_new
        acc_ref[h] = acc

    @pl.when(g == pl.num_programs(1) - 1)
    def _():
        col = jnp.sum(acc_ref[...], axis=2, keepdims=True).reshape(D_C, 1)
        acct_ref[...] = jnp.where(_iota((D_C, ns), 1) == b, col, acct_ref[...])
        ml = jnp.concatenate([m_ref[...], jnp.zeros((2 * SUBLANES - H_C, PAGE_SIZE), F32),
                              l_ref[...], jnp.zeros((2 * SUBLANES - H_C, PAGE_SIZE), F32)], axis=0)
        mlt_ref[...] = jnp.where(_iota((ML_ROWS, ns), 1) == b, ml, mlt_ref[...])


def _paged_attn(pt_flat, q_t, kt_pool, vt_pool, suf_pool, offsets, layer, ns, n_pages, *, pp=8):
    ng = n_pages // pp

    def page_map(t):
        return lambda b, g, pt: (layer, pt[b * n_pages + g * pp + t], 0, 0, 0)

    def suf_map(t):
        return lambda b, g, pt: (layer, 0, pt[b * n_pages + g * pp + t], 0, 0)

    kv_specs = [pl.BlockSpec((1, 1, H_C, HEAD_DIM, PAGE_SIZE), page_map(t)) for t in range(pp)]
    suf_specs = [pl.BlockSpec((1, H_C, 1, 1, PAGE_SIZE), suf_map(t)) for t in range(pp)]
    grid_spec = pltpu.PrefetchScalarGridSpec(
        num_scalar_prefetch=1,
        grid=(ns, ng),
        in_specs=[pl.BlockSpec((D_C, ns), lambda b, g, pt: (0, 0))] + kv_specs + kv_specs + suf_specs
        + [pl.BlockSpec((1, pp, H_C, PAGE_SIZE), lambda b, g, pt: (b, g, 0, 0))],
        out_specs=[pl.BlockSpec((D_C, ns), lambda b, g, pt: (0, 0)),
                   pl.BlockSpec((ML_ROWS, ns), lambda b, g, pt: (0, 0))],
        scratch_shapes=[
            pltpu.VMEM((H_C, HEAD_DIM, PAGE_SIZE), F32),
            pltpu.VMEM((H_C, PAGE_SIZE), F32),
            pltpu.VMEM((H_C, PAGE_SIZE), F32),
            pltpu.VMEM((H_C, HEAD_DIM, PAGE_SIZE), F32),
        ],
    )
    return pl.pallas_call(
        functools.partial(_paged_body, pp=pp, ns=ns),
        grid_spec=grid_spec,
        out_shape=[jax.ShapeDtypeStruct((D_C, ns), F32), jax.ShapeDtypeStruct((ML_ROWS, ns), F32)],
        compiler_params=_cparams("arbitrary", "arbitrary"),
        name="paged_attn",
    )(pt_flat, q_t, *([kt_pool] * pp), *([vt_pool] * pp), *([suf_pool] * pp), offsets)


def _sample_attn_post_body(acct_ref, mlt_ref, q_ref, k_ref, v_ref, y_in_ref, o_ref, *, ns):
    acc = acct_ref[...].T
    ml = jnp.concatenate([mlt_ref[...], jnp.zeros((LANES - ML_ROWS, ns), F32)], axis=0).T
    m = ml[:, 0:LANES]
    l = pltpu.roll(ml, LANES - 2 * SUBLANES, axis=1)
    seg = _seg_matrix(D_C, LANES, HEAD_DIM)
    expand = _seg_matrix(D_C, LANES, HEAD_DIM, transpose=True)
    s_self = _dot_sel_r(q_ref[...] * k_ref[...], seg)
    m_all = jnp.maximum(m, s_self)
    w_past = jnp.exp(m - m_all)
    w_self = jnp.exp(s_self - m_all)
    denom = l * w_past + w_self
    o_ref[...] = (acc * _dot_sel_r(w_past, expand) + v_ref[...] * _dot_sel_r(w_self, expand)) \
        / _dot_sel_r(denom, expand)


def _sample_attn_post(acc_t, ml_t, q, k, v, yc_all, row0):
    ns = q.shape[0]
    full = lambda r, c: pl.BlockSpec((r, c), lambda i: (0, 0))
    rows_s = pl.BlockSpec((ns, D_C), lambda i: (row0 // ns, 0))
    return pl.pallas_call(
        functools.partial(_sample_attn_post_body, ns=ns),
        grid=(1,),
        in_specs=[full(D_C, ns), full(ML_ROWS, ns), full(ns, D_C), full(ns, D_C), full(ns, D_C), rows_s],
        out_specs=rows_s,
        out_shape=jax.ShapeDtypeStruct(yc_all.shape, F32),
        input_output_aliases={5: 0},
        compiler_params=_cparams("arbitrary"),
        name="sample_attn_post",
    )(acc_t, ml_t, q, k, v, yc_all)


def _route(lg):
    lane = _iota(lg.shape, 1)
    big = jnp.int32(1 << 20)
    gl = jnp.where(lane < N_GROUPS, lg, NEG_BIG)
    gmax = jnp.max(gl, axis=1, keepdims=True)
    g_sel = jnp.min(jnp.where(gl == gmax, lane, big), axis=1, keepdims=True)
    g_w = 1.0 / jnp.sum(jnp.where(lane < N_GROUPS, jnp.exp(gl - gmax), 0.0), axis=1, keepdims=True)
    in_group = (lane >= N_GROUPS) & (lane < N_GROUPS + N_EXPERTS) & \
        (jnp.right_shift(lane - N_GROUPS, 2) == g_sel)
    el = jnp.where(in_group, lg, NEG_BIG)
    e1 = jnp.max(el, axis=1, keepdims=True)
    i1 = jnp.min(jnp.where(el == e1, lane, big), axis=1, keepdims=True)
    el2 = jnp.where(lane == i1, NEG_BIG, el)
    e2 = jnp.max(el2, axis=1, keepdims=True)
    i2 = jnp.min(jnp.where(el2 == e2, lane, big), axis=1, keepdims=True)
    r = jnp.exp(e2 - e1)
    w1 = g_w / (1.0 + r)
    w2 = g_w * r / (1.0 + r)
    out = jnp.where(lane == 0, (i1 - N_GROUPS).astype(F32), 0.0)
    out = jnp.where(lane == 1, (i2 - N_GROUPS).astype(F32), out)
    out = jnp.where(lane == 2, w1, out)
    out = jnp.where(lane == 3, w2, out)
    return out


def _out_proj_body(x_ref, ya_ref, yb_ref, yc_ref, wah_ref, wbh_ref, wch_ref, wal_ref, wbl_ref, wcl_ref,
                   g_ref, wr_ref, rb_ref, x1_ref, h2_ref, route_ref, *, precise_tiles):
    precise = _is_in(pl.program_id(0), precise_tiles)
    ys = (ya_ref, yb_ref, yc_ref)
    whs = (wah_ref, wbh_ref, wch_ref)
    wls = (wal_ref, wbl_ref, wcl_ref)

    @pl.when(jnp.logical_not(precise))
    def _():
        d = _dot(ys[0][...].astype(BF16), whs[0][...])
        for y, wh in zip(ys[1:], whs[1:]):
            d = d + _dot(y[...].astype(BF16), wh[...])
        x1_ref[...] = x_ref[...] + d

    @pl.when(precise)
    def _():
        d = jnp.zeros(x1_ref.shape, F32)
        for y, wh, wl in zip(ys, whs, wls):
            yh, yl = _split2(y[...])
            d = d + _dot(yh, wh[...]) + _dot(yl, wh[...]) + _dot(yh, wl[...])
        x1_ref[...] = x_ref[...] + d

    x1 = x1_ref[...]
    ms = jnp.mean(x1 * x1, axis=-1, keepdims=True)
    h2 = x1 * lax.rsqrt(ms + EPS) * g_ref[...]
    h2_ref[...] = h2.astype(BF16)
    route_ref[...] = _route(_dg3(h2, wr_ref[...]) + rb_ref[...])


def _out_proj(x, ya, yb, yc, wo_hi, wo_lo, g, wr, rb, precise_tiles, *, tm):
    t = x.shape[0]
    row = lambda w: pl.BlockSpec((tm, w), lambda i: (i, 0))
    const = lambda r, c: _const_spec((r, c), 1)
    parts = lambda w: (w[0:D_A], w[D_A:D_A + D_B], w[D_A + D_B:])
    wspecs = [const(D_A, D_MODEL), const(D_B, D_MODEL), const(D_C, D_MODEL)]
    return pl.pallas_call(
        functools.partial(_out_proj_body, precise_tiles=precise_tiles),
        grid=(t // tm,),
        in_specs=[row(D_MODEL), row(D_A), row(D_B), row(D_C)] + wspecs + wspecs
        + [const(1, D_MODEL), const(D_MODEL, LANES), const(1, LANES)],
        out_specs=[row(D_MODEL), row(D_MODEL), row(LANES)],
        out_shape=[
            jax.ShapeDtypeStruct((t, D_MODEL), F32),
            jax.ShapeDtypeStruct((t, D_MODEL), BF16),
            jax.ShapeDtypeStruct((t, LANES), F32),
        ],
        compiler_params=_cparams("parallel"),
        name="out_proj",
    )(x, ya, yb, yc, *parts(wo_hi), *parts(wo_lo), g, wr, rb)


MOE_TM = 256


def _moe_body(te_ref, nt_ref, xs_ref, wg_ref, wu_ref, wd_ref, ys_ref, wg16, wu16, wd16):
    i = pl.program_id(0)
    changed = te_ref[i] != te_ref[jnp.maximum(i - 1, 0)]

    @pl.when((i == 0) | changed)
    def _():
        wg16[...] = wg_ref[0].astype(BF16)
        wu16[...] = wu_ref[0].astype(BF16)
        wd16[...] = wd_ref[0].astype(BF16)

    @pl.when(i < nt_ref[0])
    def _():
        x = xs_ref[...]
        act = _silu(_dot(x, wg16[...])) * _dot(x, wu16[...])
        ys_ref[...] = _dot(act.astype(BF16), wd16[...])

    @pl.when(i >= nt_ref[0])
    def _():
        ys_ref[...] = jnp.zeros(ys_ref.shape, F32)


def _moe_gemm(tile_expert, n_tiles, xs, wg, wu, wd, layer):
    p_rows = xs.shape[0]
    tm = MOE_TM
    grid_spec = pltpu.PrefetchScalarGridSpec(
        num_scalar_prefetch=2,
        grid=(p_rows // tm,),
        in_specs=[
            pl.BlockSpec((tm, D_MODEL), lambda i, te, nt: (i, 0)),
            pl.BlockSpec((None, 1, D_MODEL, D_FF), lambda i, te, nt: (layer, te[i], 0, 0)),
            pl.BlockSpec((None, 1, D_MODEL, D_FF), lambda i, te, nt: (layer, te[i], 0, 0)),
            pl.BlockSpec((None, 1, D_FF, D_MODEL), lambda i, te, nt: (layer, te[i], 0, 0)),
        ],
        out_specs=pl.BlockSpec((tm, D_MODEL), lambda i, te, nt: (i, 0)),
        scratch_shapes=[
            pltpu.VMEM((D_MODEL, D_FF), BF16),
            pltpu.VMEM((D_MODEL, D_FF), BF16),
            pltpu.VMEM((D_FF, D_MODEL), BF16),
        ],
    )
    return pl.pallas_call(
        _moe_body,
        grid_spec=grid_spec,
        out_shape=jax.ShapeDtypeStruct((p_rows, D_MODEL), F32),
        compiler_params=_cparams("arbitrary"),
        name="moe_gemm",
    )(tile_expert, n_tiles, xs, wg, wu, wd)


def _moe_plan(route, tm):
    t = route.shape[0]
    n_assign = 2 * t
    i32 = jnp.int32
    ids = route[:, 0:2].astype(i32).reshape(-1)
    experts = jnp.arange(N_EXPERTS, dtype=i32)[None, :]
    onehot = ids[:, None] == experts
    csum = jnp.cumsum(onehot.astype(i32), axis=0)
    counts = csum[-1]
    padded = ((counts + tm - 1) // tm) * tm
    pend = jnp.cumsum(padded)
    pstart = pend - padded
    start = jnp.cumsum(counts) - counts
    pick = lambda table, hot: jnp.sum(jnp.where(hot, table[None, :], 0), axis=1)
    pos = pick(pstart, onehot) + jnp.sum(jnp.where(onehot, csum, 0), axis=1) - 1
    p_rows = ((n_assign + N_EXPERTS * tm + tm - 1) // tm) * tm
    p = jnp.arange(p_rows, dtype=i32)
    e_p = jnp.minimum(jnp.sum(p[:, None] >= pend[None, :], axis=1), N_EXPERTS - 1).astype(i32)
    hot_p = e_p[:, None] == experts
    off = p - pick(pstart, hot_p)
    order = jnp.argsort(ids, stable=True).astype(i32)
    src = order[jnp.clip(pick(start, hot_p) + off, 0, n_assign - 1)]
    src_tok = jnp.where(off < pick(counts, hot_p), src // 2, 0)
    n_tiles = (pend[-1] // tm).astype(i32)
    tile_e = e_p[::tm]
    tile_e = jnp.minimum(tile_e, tile_e[jnp.maximum(n_tiles - 1, 0)])
    return src_tok, pos.reshape(t, 2), tile_e, n_tiles.reshape(1)


def _moe_dense3_body(x1_ref, g_ref, route_ref, wg_ref, wu_ref, wd_ref, x_in_ref, o_ref):
    e = pl.program_id(0)

    @pl.when(e == 0)
    def _():
        o_ref[...] = x1_ref[...]

    x1 = x1_ref[...]
    ms = jnp.mean(x1 * x1, axis=-1, keepdims=True)
    h2 = x1 * lax.rsqrt(ms + EPS) * g_ref[...]
    route = route_ref[...]
    ef = e.astype(F32)
    gate = jnp.where(route[:, 0:1] == ef, route[:, 2:3], 0.0) + jnp.where(route[:, 1:2] == ef, route[:, 3:4], 0.0)
    act = _silu(_dg3(h2, wg_ref[0])) * _dg3(h2, wu_ref[0])
    o_ref[...] += gate * _dg3(act, wd_ref[0])


def _moe_dense3(x1, row0, ns, g, route, wg, wu, wd, layer, x_all):
    rows = pl.BlockSpec((ns, D_MODEL), lambda e: (row0 // ns, 0))
    return pl.pallas_call(
        _moe_dense3_body,
        grid=(N_EXPERTS,),
        in_specs=[
            rows,
            pl.BlockSpec((1, D_MODEL), lambda e: (0, 0)),
            pl.BlockSpec((ns, LANES), lambda e: (row0 // ns, 0)),
            pl.BlockSpec((None, 1, D_MODEL, D_FF), lambda e: (layer, e, 0, 0)),
            pl.BlockSpec((None, 1, D_MODEL, D_FF), lambda e: (layer, e, 0, 0)),
            pl.BlockSpec((None, 1, D_FF, D_MODEL), lambda e: (layer, e, 0, 0)),
            rows,
        ],
        out_specs=rows,
        out_shape=jax.ShapeDtypeStruct(x_all.shape, F32),
        input_output_aliases={6: 0},
        compiler_params=_cparams("arbitrary"),
        name="moe_dense3",
    )(x1, g, route, wg, wu, wd, x_all)


def _pad_lanes(v, n=LANES):
    return jnp.pad(v, (0, n - v.shape[0]))[None, :]


def _block_diag(w):
    h, d, _ = w.shape
    eye = jnp.eye(h, dtype=w.dtype)
    return (eye[:, None, :, None] * w[:, :, None, :]).reshape(h * d, h * d)


def _hi_lo_body(w_ref, hi_ref, lo_ref):
    hi_ref[...], lo_ref[...] = _split2(w_ref[...])


def _hi_lo(w, *, tr=256):
    rows, cols = w.shape
    spec = pl.BlockSpec((tr, cols), lambda i: (i, 0))
    out = jax.ShapeDtypeStruct((rows, cols), BF16)
    return pl.pallas_call(
        _hi_lo_body,
        grid=(rows // tr,),
        in_specs=[spec],
        out_specs=[spec, spec],
        out_shape=[out, out],
        compiler_params=_cparams("parallel"),
        name="hi_lo_split",
    )(w)


def _pack_w_in(w):
    parts, start = [], 0
    for size in PROJ_SPLITS:
        parts.append(w[:, start:start + size])
        start += size
    xa, ga, z, xbc, dt, q, k, v, f = parts
    padc = lambda a: jnp.pad(a, ((0, 0), (0, LANES - a.shape[1])))
    return _hi_lo(jnp.concatenate([z, xbc, q, k, v, padc(dt), padc(f), xa, ga], axis=1))


def _pack_router(rg, rgb, re, reb):
    wr = jnp.concatenate([rg, jnp.transpose(re, (1, 0, 2)).reshape(D_MODEL, N_EXPERTS)], axis=1)
    wr = jnp.pad(wr, ((0, 0), (0, LANES - wr.shape[1])))
    rb = _pad_lanes(jnp.concatenate([rgb, reb.reshape(-1)]))
    return wr, rb


def kernel(x_prompt, x_sample, cache_k, cache_v, cache_logf, state_lru, state_lru_conv, state_ssm,
           state_ssm_conv, page_table, norm_mix, w_in, lru_conv_w, lru_conv_b, lru_w_a, lru_b_a, lru_w_x,
           lru_b_x, lru_lambda, ssm_conv_w, ssm_conv_b, ssm_dt_bias, ssm_a_log, ssm_d, ssm_norm,
           attn_q_norm, attn_k_norm, attn_f_bias, w_out, norm_ffn, router_group, router_group_bias,
           router_expert, router_expert_bias, moe_w_gate, moe_w_up, moe_w_down):
    nb, seq, _ = x_prompt.shape
    ns = x_sample.shape[0]
    depth = w_in.shape[0]
    n_pages = page_table.shape[1]
    tp = nb * seq
    tm_in, tm_out = 640, 320
    assert (tp + ns) % tm_in == 0 and (tp + ns) % tm_out == 0 and tp % ns == 0
    precise_in = _precise_tiles(tm_in, nb, seq, ns)
    precise_out = _precise_tiles(tm_out, nb, seq, ns)

    x = jnp.concatenate([x_prompt.reshape(tp, D_MODEL), x_sample.reshape(ns, D_MODEL)], axis=0)
    kt_pool = jnp.transpose(cache_k, (0, 1, 3, 4, 2))
    vt_pool = jnp.transpose(cache_v, (0, 1, 3, 4, 2))
    n_pool = cache_logf.shape[1]
    suf_rows, tot_rows = _logf_suffix(
        jnp.transpose(cache_logf, (0, 3, 1, 2)).reshape(depth * H_C * n_pool, PAGE_SIZE))
    suf_pool = suf_rows.reshape(depth, H_C, n_pool, 1, PAGE_SIZE)
    page_tot = tot_rows[:, 0].reshape(depth, H_C, n_pool)
    conv_lru = jnp.transpose(state_lru_conv, (0, 2, 1, 3))
    conv_ssm = jnp.transpose(state_ssm_conv, (0, 2, 1, 3))
    pt_flat = page_table.reshape(-1).astype(jnp.int32)

    outs_p, outs_s = [], []
    for l in range(depth):
        precise_tail = l < depth - 1
        wi_hi, wi_lo = _pack_w_in(w_in[l])
        u = _in_proj(x, norm_mix[l][None], wi_hi, wi_lo, precise_in, tm=tm_in)

        lcb, ba, bx, lam = lru_conv_b[l][None], lru_b_a[l][None], lru_b_x[l][None], lru_lambda[l][None]
        wa, wx = _block_diag(lru_w_a[l]), _block_diag(lru_w_x[l])
        scb = ssm_conv_b[l][None]
        dtb, alog = _pad_lanes(ssm_dt_bias[l]), _pad_lanes(ssm_a_log[l])
        ng = ssm_norm[l][None]
        gq = jnp.tile(attn_q_norm[l], H_C)[None]
        gk = jnp.tile(attn_k_norm[l], H_C)[None]
        fb = _pad_lanes(attn_f_bias[l])

        ya_p, lru_h_p, lru_conv_p = _lru_prompt(u, nb, seq, lru_conv_w[l], lcb, wa, ba, wx, bx, lam)
        yb_p, ssm_h_p, ssm_conv_p = _ssd_prompt(u, nb, seq, ssm_conv_w[l], scb, dtb, alog, _pad_lanes(ssm_d[l]), ng,
                                                precise_tail=precise_tail)
        k_p, v_p, lf_p, *attn_ops = _qk_prep_prompt(u, nb, seq, gq, gk, fb, with_lo=precise_tail)
        yc_p = _fox_prompt(attn_ops, nb, seq, tp + ns, precise_tail=precise_tail)

        (ya, lru_h_s, lru_conv_s, ssm_conv_s, dtxt, dat, bm, cm, xd, zg) = _sample_pre(
            u, tp, ns, state_lru[l], conv_lru[l], conv_ssm[l], lru_conv_w[l], lcb, wa, ba, wx, bx, lam,
            ssm_conv_w[l], scb, dtb, alog, jnp.repeat(ssm_d[l], HEAD_DIM)[None], ya_p)
        ssm_h_s, yb = _ssd_update(state_ssm, l, dtxt, dat, bm, cm, xd, zg, ng, yb_p, tp)
        q_s, k_s, lf_s = _qk_prep_sample(u, tp, ns, gq, gk, fb)
        v_s = u[tp:, OFF_V:OFF_V + D_C]
        tg = page_tot[l][:, page_table]
        later_pages = jnp.flip(jnp.cumsum(jnp.flip(tg, -1), -1), -1) - tg
        off = jnp.transpose(later_pages, (1, 2, 0)) + lf_s[:, None, 0:H_C]
        offsets = jnp.broadcast_to(off[..., None], (ns, n_pages, H_C, PAGE_SIZE))
        acc_t, ml_t = _paged_attn(pt_flat, q_s.T, kt_pool, vt_pool, suf_pool, offsets, l, ns, n_pages)
        yc = _sample_attn_post(acc_t, ml_t, q_s, k_s, v_s, yc_p, tp)
        wo_hi, wo_lo = _hi_lo(w_out[l])
        wr, rb = _pack_router(router_group[l], router_group_bias[l], router_expert[l], router_expert_bias[l])
        x1, h2, route = _out_proj(x, ya, yb, yc, wo_hi, wo_lo, norm_ffn[l][None], wr, rb, precise_out, tm=tm_out)

        src_tok, pos, tile_e, n_tiles = _moe_plan(route, MOE_TM)
        ys = _moe_gemm(tile_e, n_tiles, h2[src_tok], moe_w_gate, moe_w_up, moe_w_down, l)
        x = x1 + route[:, 2:3] * ys[pos[:, 0]] + route[:, 3:4] * ys[pos[:, 1]]
        if l < depth - 1:
            x = _moe_dense3(x1, tp, ns, norm_ffn[l][None], route, moe_w_gate, moe_w_up, moe_w_down, l, x)

        outs_p.append((
            k_p.reshape(nb, seq, H_C, HEAD_DIM), v_p.reshape(nb, seq, H_C, HEAD_DIM),
            lf_p[:, 0:H_C].reshape(nb, seq, H_C), lru_h_p[:, 0], lru_conv_p, ssm_h_p, ssm_conv_p))
        outs_s.append((
            k_s.reshape(ns, 1, H_C, HEAD_DIM), v_s.reshape(ns, 1, H_C, HEAD_DIM),
            lf_s[:, 0:H_C].reshape(ns, 1, H_C), lru_h_s, jnp.transpose(lru_conv_s, (1, 0, 2)),
            ssm_h_s, jnp.transpose(ssm_conv_s, (1, 0, 2))))

    stack = lambda states, j: jnp.stack([s[j] for s in states], axis=0)
    return (x[:tp].reshape(nb, seq, D_MODEL), x[tp:].reshape(ns, 1, D_MODEL),
            *[stack(outs_p, j) for j in range(7)], *[stack(outs_s, j) for j in range(7)])
```

```python
import functools
import math

import jax
import jax.numpy as jnp
from jax import lax
from jax.experimental import pallas as pl
from jax.experimental.pallas import tpu as pltpu

F32 = jnp.float32
BF16 = jnp.bfloat16

D_MODEL = 2048
HEAD_DIM = 64
D_A = 512
H_A = 8
D_B = 768
H_B = 12
G_B = 4
N_B = 128
CONV_B = D_B + 2 * G_B * N_B
D_C = 768
H_C = 12
CONV_W = 4
LRU_C = 8.0
SSD_CHUNK = 128
PAGE_SIZE = 128
ATTN_SCALE = 1.0 / math.sqrt(HEAD_DIM)
PROJ_SPLITS = (D_A, D_A, D_B, CONV_B, H_B, D_C, D_C, D_C, H_C)
N_GROUPS = 4
E_PER_GROUP = 4
N_EXPERTS = 16
D_FF = 512
EPS = 1e-6
NEG_BIG = -1e30

LANES = 128
SUBLANES = 8
VMEM_LIMIT = 56 * 1024 * 1024

TAIL = 256
LOG2E = math.log2(math.e)

OFF_Z = 0
OFF_XBC = 768
OFF_Q = 2560
OFF_K = 3328
OFF_V = 4096
OFF_DT = 4864
OFF_F = 4992
OFF_XA = 5120
OFF_GA = 5632
N_U = 6144


def _cparams(*sem):
    return pltpu.CompilerParams(dimension_semantics=sem, vmem_limit_bytes=VMEM_LIMIT)


_NN = (((1,), (0,)), ((), ()))
_NT = (((1,), (1,)), ((), ()))
_TN = (((0,), (0,)), ((), ()))


def _dg(a, b, dims=_NN):
    return lax.dot_general(a, b, dims, preferred_element_type=F32)


def _dot(a, b):
    return _dg(a, b, _NN)


def _split2(x):
    hi = x.astype(BF16)
    lo = (x - hi.astype(F32)).astype(BF16)
    return hi, lo


def _split3(x):
    hi = x.astype(BF16)
    r = x - hi.astype(F32)
    mid = r.astype(BF16)
    lo = (r - mid.astype(F32)).astype(BF16)
    return hi, mid, lo


def _dot_sel_r(x, sel):
    hi, mid, lo = _split3(x)
    return _dot(hi, sel) + _dot(mid, sel) + _dot(lo, sel)


def _dot_sel_l(sel, x):
    hi, mid, lo = _split3(x)
    return _dot(sel, hi) + _dot(sel, mid) + _dot(sel, lo)


def _dg3(a, b, dims=_NN):
    ah, al = _split2(a)
    bh, bl = _split2(b)
    return _dg(ah, bh, dims) + _dg(al, bh, dims) + _dg(ah, bl, dims)


def _mm(a, b, dims, precise):
    if precise:
        return _dg3(a, b, dims)
    return _dg(a.astype(BF16), b.astype(BF16), dims)


def _sigmoid(x):
    return 1.0 / (1.0 + jnp.exp(-x))


def _silu(x):
    return x * _sigmoid(x)


def _log_sigmoid(x):
    return jnp.minimum(x, 0.0) - jnp.log1p(jnp.exp(-jnp.abs(x)))


def _softplus(x):
    return jnp.maximum(x, 0.0) + jnp.log1p(jnp.exp(-jnp.abs(x)))


def _gelu_tanh(x):
    return 0.5 * x * (1.0 + jnp.tanh(math.sqrt(2.0 / math.pi) * (x + 0.044715 * (x * x * x))))


def _iota(shape, dim):
    return lax.broadcasted_iota(jnp.int32, shape, dim)


def _seg_matrix(n_rows, n_cols, seg, transpose=False):
    if transpose:
        m = (_iota((n_cols, n_rows), 1) // seg) == _iota((n_cols, n_rows), 0)
    else:
        m = (_iota((n_rows, n_cols), 0) // seg) == _iota((n_rows, n_cols), 1)
    return jnp.where(m, 1.0, 0.0).astype(BF16)


def _precise_tiles(tm, nb, seq, ns):
    spans = [((b + 1) * seq - TAIL, (b + 1) * seq) for b in range(nb)] + [(nb * seq, nb * seq + ns)]
    n_tiles = (nb * seq + ns) // tm
    return tuple(i for i in range(n_tiles) if any(lo < (i + 1) * tm and hi > i * tm for lo, hi in spans))


def _is_in(i, ids):
    hit = i == ids[0]
    for t in ids[1:]:
        hit = hit | (i == t)
    return hit


def _const_spec(shape, n_grid):
    zeros = (0,) * len(shape)
    return pl.BlockSpec(shape, lambda *_: zeros, pipeline_mode=pl.Buffered(1))


def _in_proj_body(x_ref, g_ref, wh_ref, wl_ref, o_ref, xh_ref, xl_ref, *, precise_tiles):
    precise = _is_in(pl.program_id(0), precise_tiles)

    @pl.when(pl.program_id(1) == 0)
    def _():
        x = x_ref[...]
        ms = jnp.mean(x * x, axis=-1, keepdims=True)
        xh_ref[...], xl_ref[...] = _split2(x * lax.rsqrt(ms + EPS) * g_ref[...])

    @pl.when(jnp.logical_not(precise))
    def _():
        o_ref[...] = _dot(xh_ref[...], wh_ref[...])

    @pl.when(precise)
    def _():
        o_ref[...] = (_dot(xh_ref[...], wh_ref[...]) + _dot(xl_ref[...], wh_ref[...])
                      + _dot(xh_ref[...], wl_ref[...]))


def _in_proj(x, g, wh, wl, precise_tiles, *, tm, tn=1024):
    t = x.shape[0]
    return pl.pallas_call(
        functools.partial(_in_proj_body, precise_tiles=precise_tiles),
        grid=(t // tm, N_U // tn),
        in_specs=[
            pl.BlockSpec((tm, D_MODEL), lambda i, j: (i, 0)),
            pl.BlockSpec((1, D_MODEL), lambda i, j: (0, 0)),
            pl.BlockSpec((D_MODEL, tn), lambda i, j: (0, j)),
            pl.BlockSpec((D_MODEL, tn), lambda i, j: (0, jnp.where(_is_in(i, precise_tiles), j, 0))),
        ],
        out_specs=pl.BlockSpec((tm, tn), lambda i, j: (i, j)),
        out_shape=jax.ShapeDtypeStruct((t, N_U), F32),
        scratch_shapes=[pltpu.VMEM((tm, D_MODEL), BF16), pltpu.VMEM((tm, D_MODEL), BF16)],
        compiler_params=_cparams("parallel", "arbitrary"),
        name="in_proj",
    )(x, g, wh, wl)


def _lru_gates(xc, wa, ba, wx, bx, lam):
    r = _sigmoid(_dg3(xc, wa) + ba)
    i = _sigmoid(_dg3(xc, wx) + bx)
    log_a = LRU_C * r * _log_sigmoid(lam)
    a = jnp.exp(log_a)
    th = jnp.tanh(log_a)
    mult = jnp.sqrt(-2.0 * th / (1.0 - th))
    return a, mult * (i * xc)


def _lru_prompt_body(u_ref, cw_ref, cb_ref, wa_ref, ba_ref, wx_ref, bx_ref, lam_ref,
                     y_ref, h_out_ref, conv_out_ref, xp_ref, h_ref, *, tl):
    i = pl.program_id(1)

    @pl.when(i == 0)
    def _():
        xp_ref[0:SUBLANES, :] = jnp.zeros((SUBLANES, D_A), F32)
        h_ref[...] = jnp.zeros((1, D_A), F32)

    x = u_ref[:, 0:D_A]
    ga = u_ref[:, D_A:2 * D_A]
    xp_ref[SUBLANES:SUBLANES + tl, :] = x
    cw = cw_ref[...]
    xc = cb_ref[...] + cw[3:4, :] * x
    for j in range(CONV_W - 1):
        xc = xc + cw[j:j + 1, :] * xp_ref[pl.ds(SUBLANES - 3 + j, tl), :]
    xp_ref[0:SUBLANES, :] = x[tl - SUBLANES:tl, :]

    a, u = _lru_gates(xc, wa_ref[...], ba_ref[...], wx_ref[...], bx_ref[...], lam_ref[...])

    row = _iota((tl, D_A), 0)
    s = 1
    while s < tl:
        keep = row >= s
        a_sh = jnp.where(keep, pltpu.roll(a, s, axis=0), 1.0)
        u_sh = jnp.where(keep, pltpu.roll(u, s, axis=0), 0.0)
        u = u + a * u_sh
        a = a * a_sh
        s *= 2
    h = a * h_ref[...] + u
    h_ref[...] = h[tl - 1:tl, :]
    y_ref[...] = h * _gelu_tanh(ga)

    @pl.when(i == pl.num_programs(1) - 1)
    def _():
        h_out_ref[0] = h[tl - 1:tl, :]
        conv_out_ref[0] = x[tl - 3:tl, :]


def _lru_prompt(u, nb, seq, cw, cb, wa, ba, wx, bx, lam, *, tl=512):
    nt = seq // tl
    vec = lambda n: pl.BlockSpec((1, n), lambda b, i: (0, 0))
    mat = lambda r, c: pl.BlockSpec((r, c), lambda b, i: (0, 0))
    rb = lambda b, i: b * nt + i
    return pl.pallas_call(
        functools.partial(_lru_prompt_body, tl=tl),
        grid=(nb, nt),
        in_specs=[
            pl.BlockSpec((tl, 2 * D_A), lambda b, i: (rb(b, i), OFF_XA // (2 * D_A))),
            mat(CONV_W, D_A), vec(D_A), mat(D_A, D_A), vec(D_A), mat(D_A, D_A), vec(D_A), vec(D_A),
        ],
        out_specs=[
            pl.BlockSpec((tl, D_A), lambda b, i: (rb(b, i), 0)),
            pl.BlockSpec((1, 1, D_A), lambda b, i: (b, 0, 0)),
            pl.BlockSpec((1, CONV_W - 1, D_A), lambda b, i: (b, 0, 0)),
        ],
        out_shape=[
            jax.ShapeDtypeStruct((u.shape[0], D_A), F32),
            jax.ShapeDtypeStruct((nb, 1, D_A), F32),
            jax.ShapeDtypeStruct((nb, CONV_W - 1, D_A), F32),
        ],
        scratch_shapes=[pltpu.VMEM((SUBLANES + tl, D_A), F32), pltpu.VMEM((1, D_A), F32)],
        compiler_params=_cparams("parallel", "arbitrary"),
        name="lru_prompt",
    )(u, cw, cb, wa, ba, wx, bx, lam)


def _group_rmsnorm(y, g):
    gw = D_B // G_B
    ss = _dot_sel_r(y * y, _seg_matrix(D_B, LANES, gw))
    rs = lax.rsqrt(ss * (1.0 / gw) + EPS)
    return y * _dot_sel_r(rs, _seg_matrix(D_B, LANES, gw, transpose=True)) * g


def _ssd_chunk(x, bm, cm, dt, dta, dsk, h_ref, precise):
    q = SSD_CHUNK
    tri = jnp.where(_iota((q, q), 0) >= _iota((q, q), 1), 1.0, 0.0).astype(BF16)
    cum = _dot_sel_l(tri, dta)
    cum_t = cum.T
    dt_t = dt.T
    causal = _iota((q, q), 0) >= _iota((q, q), 1)
    ys = []
    for g in range(G_B):
        bg = bm[:, g * N_B:(g + 1) * N_B]
        cg = cm[:, g * N_B:(g + 1) * N_B]
        scores = _mm(cg, bg, _NT, precise)
        for r in range(H_B // G_B):
            h = g * (H_B // G_B) + r
            xh = x[:, h * HEAD_DIM:(h + 1) * HEAD_DIM]
            cl = cum[:, h:h + 1]
            cs = cum_t[h:h + 1, :]
            c_last = cum_t[h:h + 1, q - 1:q]
            decay = jnp.exp(jnp.where(causal, cl - cs, NEG_BIG))
            w = scores * decay * dt_t[h:h + 1, :]
            y_diag = _mm(w, xh, _NN, precise)
            h_prev = h_ref[h]
            y_off = _mm(cg, h_prev, _NT, precise) * jnp.exp(cl)
            de = jnp.exp(c_last - cl) * dt[:, h:h + 1]
            st = _mm(xh * de, bg, _TN, precise)
            h_ref[h] = jnp.exp(c_last) * h_prev + st
            ys.append(y_diag + y_off + dsk[:, h:h + 1] * xh)
    return jnp.concatenate(ys, axis=1)


def _ssd_prompt_body(u_ref, s_ref, cw_ref, cb_ref, dtb_ref, alog_ref, dsk_ref, ng_ref,
                     y_ref, h_out_ref, conv_out_ref, xp_ref, h_ref, *, precise_tail):
    c = pl.program_id(1)
    nc = pl.num_programs(1)
    q = SSD_CHUNK

    @pl.when(c == 0)
    def _():
        xp_ref[0:SUBLANES, :] = jnp.zeros((SUBLANES, CONV_B), F32)
        h_ref[...] = jnp.zeros((H_B, HEAD_DIM, N_B), F32)

    z = u_ref[:, OFF_Z:OFF_Z + D_B]
    xbc = u_ref[:, OFF_XBC:OFF_XBC + CONV_B]
    xp_ref[SUBLANES:SUBLANES + q, :] = xbc
    cw = cw_ref[...]
    xc = cb_ref[...] + cw[3:4, :] * xbc
    for j in range(CONV_W - 1):
        xc = xc + cw[j:j + 1, :] * xp_ref[pl.ds(SUBLANES - 3 + j, q), :]
    xp_ref[0:SUBLANES, :] = xbc[q - SUBLANES:q, :]
    xc = _silu(xc)
    x = xc[:, 0:D_B]
    bm = xc[:, D_B:D_B + G_B * N_B]
    cm = xc[:, D_B + G_B * N_B:]
    dt = _softplus(s_ref[:, 0:LANES] + dtb_ref[...])
    dta = dt * (-jnp.exp(alog_ref[...]))
    gate = _silu(z)

    def finish(precise):
        y = _ssd_chunk(x, bm, cm, dt, dta, dsk_ref[...], h_ref, precise) * gate
        y_ref[...] = _group_rmsnorm(y, ng_ref[...])

    if precise_tail:
        in_tail = c >= nc - TAIL // q
        pl.when(in_tail)(lambda: finish(True))
        pl.when(jnp.logical_not(in_tail))(lambda: finish(False))
    else:
        finish(False)

    @pl.when(c == nc - 1)
    def _():
        h_out_ref[0] = h_ref[...]
        conv_out_ref[0] = xbc[q - 3:q, :]


def _ssd_prompt(u, nb, seq, cw, cb, dtb, alog, dsk, ng, *, precise_tail):
    nc = seq // SSD_CHUNK
    q = SSD_CHUNK
    vec = lambda n: pl.BlockSpec((1, n), lambda b, c: (0, 0))
    rb = lambda b, c: b * nc + c
    return pl.pallas_call(
        functools.partial(_ssd_prompt_body, precise_tail=precise_tail),
        grid=(nb, nc),
        in_specs=[
            pl.BlockSpec((q, OFF_Q), lambda b, c: (rb(b, c), 0)),
            pl.BlockSpec((q, 2 * LANES), lambda b, c: (rb(b, c), OFF_DT // (2 * LANES))),
            pl.BlockSpec((CONV_W, CONV_B), lambda b, c: (0, 0)),
            vec(CONV_B), vec(LANES), vec(LANES), vec(LANES), vec(D_B),
        ],
        out_specs=[
            pl.BlockSpec((q, D_B), lambda b, c: (rb(b, c), 0)),
            pl.BlockSpec((1, H_B, HEAD_DIM, N_B), lambda b, c: (b, 0, 0, 0)),
            pl.BlockSpec((1, CONV_W - 1, CONV_B), lambda b, c: (b, 0, 0)),
        ],
        out_shape=[
            jax.ShapeDtypeStruct((u.shape[0], D_B), F32),
            jax.ShapeDtypeStruct((nb, H_B, HEAD_DIM, N_B), F32),
            jax.ShapeDtypeStruct((nb, CONV_W - 1, CONV_B), F32),
        ],
        scratch_shapes=[pltpu.VMEM((SUBLANES + q, CONV_B), F32),
                        pltpu.VMEM((H_B, HEAD_DIM, N_B), F32)],
        compiler_params=_cparams("parallel", "arbitrary"),
        name="ssd_prompt",
    )(u, u, cw, cb, dtb, alog, dsk, ng)


def _head_rmsnorm(x, g):
    ss = _dot_sel_r(x * x, _seg_matrix(D_C, LANES, HEAD_DIM))
    rs = lax.rsqrt(ss * (1.0 / HEAD_DIM) + EPS)
    return x * _dot_sel_r(rs, _seg_matrix(D_C, LANES, HEAD_DIM, transpose=True)) * g


D_CX = H_C * LANES


def _widen_heads(x, extras):
    parts = []
    for h in range(H_C):
        parts += [x[:, h * HEAD_DIM:(h + 1) * HEAD_DIM], extras[h]]
    return jnp.concatenate(parts, axis=1)


def _qk_prep_prompt_body(u_ref, gq_ref, gk_ref, fb_ref, k_ref, v_ref, lf_ref, qh_ref, kh_ref, vh_ref,
                         *rest, tl, with_lo):
    if with_lo:
        ql_ref, kl_ref, vl_ref, carry_ref = rest
    else:
        (carry_ref,) = rest
    q = u_ref[:, 0:D_C]
    k = u_ref[:, D_C:2 * D_C]
    v = u_ref[:, 2 * D_C:3 * D_C]
    f_raw = u_ref[:, OFF_F - OFF_Q:OFF_F - OFF_Q + LANES]
    qn = _head_rmsnorm(q, gq_ref[...]) * (ATTN_SCALE * LOG2E)
    kn = _head_rmsnorm(k, gk_ref[...])
    k_ref[...] = kn
    v_ref[...] = v
    lf = _log_sigmoid(f_raw + fb_ref[...])
    lf_ref[...] = lf

    @pl.when(pl.program_id(1) == 0)
    def _():
        carry_ref[...] = jnp.zeros((1, LANES), F32)

    tri = jnp.where(_iota((tl, tl), 0) >= _iota((tl, tl), 1), 1.0, 0.0).astype(BF16)
    c = _dot_sel_l(tri, lf) + carry_ref[...]
    carry_ref[...] = c[tl - 1:tl, :]
    b3 = [p.astype(F32) for p in _split3(c * (-LOG2E))]

    lane = _iota((tl, HEAD_DIM), 1)
    zeros = jnp.zeros((tl, HEAD_DIM), F32)
    ones = jnp.ones((tl, HEAD_DIM), F32)
    sel3 = jnp.where(lane < 3, 1.0, 0.0)
    bias = [jnp.where(lane == 0, b3[0][:, h:h + 1],
                      jnp.where(lane == 1, b3[1][:, h:h + 1],
                                jnp.where(lane == 2, b3[2][:, h:h + 1], 0.0))) for h in range(H_C)]

    def hi_lo(x):
        hi = x.astype(BF16)
        return hi, x - hi.astype(F32)

    q_hi, q_rem = hi_lo(qn)
    k_hi, k_rem = hi_lo(kn)
    v_hi, v_rem = hi_lo(v)
    qh_ref[...] = _widen_heads(q_hi.astype(F32), [sel3] * H_C).astype(BF16)
    kh_ref[...] = _widen_heads(k_hi.astype(F32), bias).astype(BF16)
    vh_ref[0] = _widen_heads(v_hi.astype(F32), [ones] * H_C).T.astype(BF16)
    if with_lo:
        ql_ref[...] = _widen_heads(q_rem, [zeros] * H_C).astype(BF16)
        kl_ref[...] = _widen_heads(k_rem, [zeros] * H_C).astype(BF16)
        vl_ref[0] = _widen_heads(v_rem, [zeros] * H_C).T.astype(BF16)


def _qk_prep_prompt(u, nb, seq, gq, gk, fb, *, with_lo, tl=512):
    nt = seq // tl
    rows = nb * seq
    wq = OFF_XA - OFF_Q
    vec = lambda n: pl.BlockSpec((1, n), lambda b, i: (0, 0))
    nat = lambda w: pl.BlockSpec((tl, w), lambda b, i: (b * nt + i, 0))
    sx = jax.ShapeDtypeStruct((rows, D_CX), BF16)
    s32 = jax.ShapeDtypeStruct((rows, D_C), F32)
    vt_spec = pl.BlockSpec((1, D_CX, tl), lambda b, i: (b, 0, i))
    sxt = jax.ShapeDtypeStruct((nb, D_CX, seq), BF16)
    reps = 2 if with_lo else 1
    return pl.pallas_call(
        functools.partial(_qk_prep_prompt_body, tl=tl, with_lo=with_lo),
        grid=(nb, nt),
        in_specs=[
            pl.BlockSpec((tl, wq), lambda b, i: (b * nt + i, OFF_Q // wq)),
            vec(D_C), vec(D_C), vec(LANES),
        ],
        out_specs=[nat(D_C), nat(D_C), nat(LANES)] + [nat(D_CX), nat(D_CX), vt_spec] * reps,
        out_shape=[s32, s32, jax.ShapeDtypeStruct((rows, LANES), F32)] + [sx, sx, sxt] * reps,
        scratch_shapes=[pltpu.VMEM((1, LANES), F32)],
        compiler_params=_cparams("parallel", "arbitrary"),
        name="qk_prep_prompt",
    )(u, gq, gk, fb)


def _qk_prep_sample_body(u_ref, gq_ref, gk_ref, fb_ref, q_ref, k_ref, lf_ref):
    q = u_ref[:, 0:D_C]
    k = u_ref[:, D_C:2 * D_C]
    f_raw = u_ref[:, OFF_F - OFF_Q:OFF_F - OFF_Q + LANES]
    q_ref[...] = _head_rmsnorm(q, gq_ref[...]) * ATTN_SCALE
    k_ref[...] = _head_rmsnorm(k, gk_ref[...])
    lf_ref[...] = _log_sigmoid(f_raw + fb_ref[...])


def _qk_prep_sample(u, row0, ns, gq, gk, fb):
    wq = OFF_XA - OFF_Q
    full = lambda r, c: pl.BlockSpec((r, c), lambda i: (0, 0))
    return pl.pallas_call(
        _qk_prep_sample_body,
        grid=(1,),
        in_specs=[pl.BlockSpec((ns, wq), lambda i: (row0 // ns, OFF_Q // wq)),
                  full(1, D_C), full(1, D_C), full(1, LANES)],
        out_specs=[full(ns, D_C), full(ns, D_C), full(ns, LANES)],
        out_shape=[jax.ShapeDtypeStruct((ns, D_C), F32), jax.ShapeDtypeStruct((ns, D_C), F32),
                   jax.ShapeDtypeStruct((ns, LANES), F32)],
        compiler_params=_cparams("arbitrary"),
        name="qk_prep_sample",
    )(u, gq, gk, fb)


def _fox_tile(i, refs, tq, precise):
    if precise:
        qh_ref, kh_ref, vh_ref, ql_ref, kl_ref, vl_ref, o_ref = refs
    else:
        qh_ref, kh_ref, vh_ref, o_ref = refs
    causal_t = _iota((tq, tq), 0) <= _iota((tq, tq), 1)
    heads = [slice(hh * LANES, (hh + 1) * LANES) for hh in range(2)]
    qh = [qh_ref[:, sl] for sl in heads]
    ql = [ql_ref[:, sl] for sl in heads] if precise else None

    def scores(j, hh):
        off = pl.multiple_of(j * tq, tq)
        kh = kh_ref[pl.ds(off, tq), heads[hh]]
        s = _dg(kh, qh[hh], _NT)
        if precise:
            s = s + _dg(kh, ql[hh], _NT) + _dg(kl_ref[pl.ds(off, tq), heads[hh]], qh[hh], _NT)
        return s

    def update(j, hh, state, s, masked):
        m, acc = state
        off = pl.multiple_of(j * tq, tq)
        vh = vh_ref[0, heads[hh], pl.ds(off, tq)]
        if masked:
            s = jnp.where(causal_t, s, NEG_BIG)
        m_new = jnp.maximum(m, jnp.max(s, axis=0, keepdims=True))
        alpha = jnp.exp2(m - m_new)
        p = jnp.exp2(s - m_new)
        if precise:
            p_hi, p_lo = _split2(p)
            pv = _dot(vh, p_hi) + _dot(vh, p_lo) + _dot(vl_ref[0, heads[hh], pl.ds(off, tq)], p_hi)
        else:
            pv = _dot(vh, p.astype(BF16))
        return m_new, alpha * acc + pv

    def step(j, carry):
        out = []
        for hh in range(2):
            state, s_cur = carry[hh]
            s_next = scores(j + 1, hh)
            out.append((update(j, hh, state, s_cur, False), s_next))
        return tuple(out)

    init = tuple(((jnp.full((1, tq), NEG_BIG, F32), jnp.zeros((LANES, tq), F32)), scores(0, hh))
                 for hh in range(2))
    carry = lax.fori_loop(0, i, step, init)
    carry = tuple(update(i, hh, state, s_cur, True) for hh, (state, s_cur) in enumerate(carry))
    o_t = jnp.concatenate([acc[0:HEAD_DIM] / acc[HEAD_DIM:HEAD_DIM + 1] for _, acc in carry], axis=0)
    o_ref[...] = o_t.T


def _fox_body(*refs, tq, precise_tail):
    i = pl.program_id(2)
    if precise_tail:
        last = i == pl.num_programs(2) - 1
        plain = refs[0:3] + refs[6:]
        pl.when(last)(lambda: _fox_tile(i, refs, tq, True))
        pl.when(jnp.logical_not(last))(lambda: _fox_tile(i, plain, tq, False))
    else:
        _fox_tile(i, refs, tq, False)


def _fox_prompt(operands, nb, seq, total_rows, *, precise_tail, tq=512):
    nq = seq // tq
    npair = H_C // 2
    qspec = pl.BlockSpec((tq, 2 * LANES), lambda b, p, i: (b * nq + i, p))
    kspec = pl.BlockSpec((seq, 2 * LANES), lambda b, p, i: (b, p))
    vspec = pl.BlockSpec((1, 2 * LANES, seq), lambda b, p, i: (b, p, 0))
    return pl.pallas_call(
        functools.partial(_fox_body, tq=tq, precise_tail=precise_tail),
        grid=(nb, npair, nq),
        in_specs=[qspec, kspec, vspec] * (2 if precise_tail else 1),
        out_specs=pl.BlockSpec((tq, LANES), lambda b, p, i: (b * nq + i, p)),
        out_shape=jax.ShapeDtypeStruct((total_rows, D_C), F32),
        compiler_params=_cparams("parallel", "parallel", "arbitrary"),
        name="fox_prompt",
    )(*operands)


def _sample_pre_body(u_ref, hl_ref, cl_ref, cs_ref, lcw_ref, lcb_ref, wa_ref, ba_ref, wx_ref, bx_ref,
                     lam_ref, scw_ref, scb_ref, dtb_ref, alog_ref, dsk_ref, ya_in_ref,
                     ya_ref, hnew_ref, lconv_ref, sconv_ref, dtxt_ref, dat_ref, bm_ref, cm_ref,
                     xd_ref, zg_ref):
    xa = u_ref[:, OFF_XA:OFF_XA + D_A]
    ga = u_ref[:, OFF_GA:OFF_GA + D_A]
    cw = lcw_ref[...]
    xc = lcb_ref[...] + cw[3:4, :] * xa
    for j in range(CONV_W - 1):
        xc = xc + cw[j:j + 1, :] * cl_ref[j]
    lconv_ref[0] = cl_ref[1]
    lconv_ref[1] = cl_ref[2]
    lconv_ref[2] = xa
    a, uu = _lru_gates(xc, wa_ref[...], ba_ref[...], wx_ref[...], bx_ref[...], lam_ref[...])
    h = a * hl_ref[...] + uu
    hnew_ref[...] = h
    ya_ref[...] = h * _gelu_tanh(ga)

    z = u_ref[:, OFF_Z:OFF_Z + D_B]
    xbc = u_ref[:, OFF_XBC:OFF_XBC + CONV_B]
    cw = scw_ref[...]
    xs = scb_ref[...] + cw[3:4, :] * xbc
    for j in range(CONV_W - 1):
        xs = xs + cw[j:j + 1, :] * cs_ref[j]
    sconv_ref[0] = cs_ref[1]
    sconv_ref[1] = cs_ref[2]
    sconv_ref[2] = xbc
    xs = _silu(xs)
    x = xs[:, 0:D_B]
    bm_ref[...] = xs[:, D_B:D_B + G_B * N_B]
    cm_ref[...] = xs[:, D_B + G_B * N_B:]
    dt = _softplus(u_ref[:, OFF_DT:OFF_DT + LANES] + dtb_ref[...])
    da = jnp.exp(dt * (-jnp.exp(alog_ref[...])))
    expand = _seg_matrix(D_B, LANES, HEAD_DIM, transpose=True)
    dtxt_ref[...] = (_dot_sel_r(dt, expand) * x).T
    dat_ref[...] = _dot_sel_r(da, expand).T
    xd_ref[...] = dsk_ref[...] * x
    zg_ref[...] = _silu(z)


def _sample_pre(u, row0, ns, h_lru, conv_lru, conv_ssm, lcw, lcb, wa, ba, wx, bx, lam, scw, scb, dtb, alog, dsk_e,
                ya_all):
    full = lambda *shape: pl.BlockSpec(shape, lambda i: (0,) * len(shape))
    rows_s = lambda w: pl.BlockSpec((ns, w), lambda i: (row0 // ns, 0))
    f = lambda *shape: jax.ShapeDtypeStruct(shape, F32)
    return pl.pallas_call(
        _sample_pre_body,
        grid=(1,),
        in_specs=[
            pl.BlockSpec((ns, N_U), lambda i: (row0 // ns, 0)),
            full(ns, D_A), full(CONV_W - 1, ns, D_A), full(CONV_W - 1, ns, CONV_B),
            full(CONV_W, D_A), full(1, D_A), full(D_A, D_A), full(1, D_A), full(D_A, D_A), full(1, D_A),
            full(1, D_A), full(CONV_W, CONV_B), full(1, CONV_B), full(1, LANES), full(1, LANES), full(1, D_B),
            rows_s(D_A),
        ],
        out_specs=[
            rows_s(D_A), full(ns, D_A), full(CONV_W - 1, ns, D_A), full(CONV_W - 1, ns, CONV_B),
            full(D_B, ns), full(D_B, ns), full(ns, G_B * N_B), full(ns, G_B * N_B), full(ns, D_B), full(ns, D_B),
        ],
        out_shape=[
            f(*ya_all.shape), f(ns, D_A), f(CONV_W - 1, ns, D_A), f(CONV_W - 1, ns, CONV_B),
            f(D_B, ns), f(D_B, ns), f(ns, G_B * N_B), f(ns, G_B * N_B), f(ns, D_B), f(ns, D_B),
        ],
        input_output_aliases={16: 0},
        compiler_params=_cparams("arbitrary"),
        name="sample_pre",
    )(u, h_lru, conv_lru, conv_ssm, lcw, lcb, wa, ba, wx, bx, lam, scw, scb, dtb, alog, dsk_e, ya_all)


def _lane_bcast_column(parts, b, ns):
    onehot = jnp.where(_iota((ns, LANES), 0) == b, 1.0, 0.0).astype(BF16)
    out = _dot(parts[0], onehot)
    for p in parts[1:]:
        out = out + _dot(p, onehot)
    return out


def _ssd_update_body(st_ref, dtxt_ref, dat_ref, bm_ref, cm_ref, xd_ref, zg_ref, ng_ref, y_in_ref,
                     st_out_ref, y_ref, yt_ref, *, tb, ns):
    i = pl.program_id(0)

    @pl.when(i == 0)
    def _():
        yt_ref[...] = jnp.zeros((D_B, ns), F32)

    dtx3 = _split3(dtxt_ref[...])
    da3 = _split3(dat_ref[...])
    lane = _iota((D_B, ns), 1)
    rows_per_group = D_B // G_B

    def group_rows(row):
        return jnp.concatenate(
            [jnp.broadcast_to(row[:, g * N_B:(g + 1) * N_B], (rows_per_group, N_B)) for g in range(G_B)], axis=0)

    for t in range(tb):
        b = i * tb + t
        xb = _lane_bcast_column(dtx3, b, ns)
        dab = _lane_bcast_column(da3, b, ns)
        bexp = group_rows(bm_ref[pl.ds(b, 1), :])
        cexp = group_rows(cm_ref[pl.ds(b, 1), :])
        s_new = dab * st_ref[t].reshape(D_B, N_B) + xb * bexp
        st_out_ref[t] = s_new.reshape(H_B, HEAD_DIM, N_B)
        ycol = jnp.sum(s_new * cexp, axis=1, keepdims=True)
        yt_ref[...] = jnp.where(lane == b, ycol, yt_ref[...])

    @pl.when(i == pl.num_programs(0) - 1)
    def _():
        y = (yt_ref[...].T + xd_ref[...]) * zg_ref[...]
        y_ref[...] = _group_rmsnorm(y, ng_ref[...])


def _ssd_update(state, layer, dtxt, dat, bm, cm, xd, zg, ng, yb_all, row0, *, tb=8):
    ns = state.shape[1]
    full = lambda *shape: pl.BlockSpec(shape, lambda i: (0,) * len(shape))
    rows_s = pl.BlockSpec((ns, D_B), lambda i: (row0 // ns, 0))
    return pl.pallas_call(
        functools.partial(_ssd_update_body, tb=tb, ns=ns),
        grid=(ns // tb,),
        in_specs=[
            pl.BlockSpec((None, tb, H_B, HEAD_DIM, N_B), lambda i: (layer, i, 0, 0, 0)),
            full(D_B, ns), full(D_B, ns), full(ns, G_B * N_B), full(ns, G_B * N_B),
            full(ns, D_B), full(ns, D_B), full(1, D_B), rows_s,
        ],
        out_specs=[
            pl.BlockSpec((tb, H_B, HEAD_DIM, N_B), lambda i: (i, 0, 0, 0)),
            rows_s,
        ],
        out_shape=[
            jax.ShapeDtypeStruct((ns, H_B, HEAD_DIM, N_B), F32),
            jax.ShapeDtypeStruct(yb_all.shape, F32),
        ],
        scratch_shapes=[pltpu.VMEM((D_B, ns), F32)],
        input_output_aliases={8: 1},
        compiler_params=_cparams("arbitrary"),
        name="ssd_update",
    )(state, dtxt, dat, bm, cm, xd, zg, ng, yb_all)


def _logf_suffix_body(lf_ref, suf_ref, tot_ref):
    later = jnp.where(_iota((PAGE_SIZE, PAGE_SIZE), 0) > _iota((PAGE_SIZE, PAGE_SIZE), 1), 1.0, 0.0).astype(BF16)
    lf = lf_ref[...]
    suf_ref[...] = _dot_sel_r(lf, later)
    tot_ref[...] = _dot_sel_r(lf, jnp.ones((PAGE_SIZE, PAGE_SIZE), BF16))


def _logf_suffix(lf_rows, *, tr=2048):
    rows = lf_rows.shape[0]
    spec = pl.BlockSpec((tr, PAGE_SIZE), lambda i: (i, 0))
    out = jax.ShapeDtypeStruct((rows, PAGE_SIZE), F32)
    return pl.pallas_call(
        _logf_suffix_body,
        grid=(rows // tr,),
        in_specs=[spec],
        out_specs=[spec, spec],
        out_shape=[out, out],
        compiler_params=_cparams("parallel"),
        name="logf_suffix",
    )(lf_rows)


ML_ROWS = 4 * SUBLANES


def _paged_body(pt_ref, qt_ref, *refs, pp, ns):
    k_refs = refs[0:pp]
    v_refs = refs[pp:2 * pp]
    suf_refs = refs[2 * pp:3 * pp]
    off_ref, acct_ref, mlt_ref, qb_ref, m_ref, l_ref, acc_ref = refs[3 * pp:]
    b = pl.program_id(0)
    g = pl.program_id(1)

    @pl.when(g == 0)
    def _():
        qb_ref[...] = _lane_bcast_column(_split3(qt_ref[...]), b, ns).reshape(H_C, HEAD_DIM, PAGE_SIZE)
        m_ref[...] = jnp.full((H_C, PAGE_SIZE), NEG_BIG, F32)
        l_ref[...] = jnp.zeros((H_C, PAGE_SIZE), F32)
        acc_ref[...] = jnp.zeros((H_C, HEAD_DIM, PAGE_SIZE), F32)

    @pl.when((b == 0) & (g == 0))
    def _():
        acct_ref[...] = jnp.zeros(acct_ref.shape, F32)
        mlt_ref[...] = jnp.zeros(mlt_ref.shape, F32)

    for h in range(H_C):
        qh = qb_ref[h]
        s = [jnp.sum(qh * k_refs[t][0, 0, h], axis=0, keepdims=True)
             + suf_refs[t][0, h, 0] + off_ref[0, t, h:h + 1, :] for t in range(pp)]
        m_old = m_ref[h:h + 1, :]
        m_new = m_old
        for st in s:
            m_new = jnp.maximum(m_new, jnp.max(st, axis=1, keepdims=True))
        alpha = jnp.exp(m_old - m_new)
        p = [jnp.exp(st - m_new) for st in s]
        l_new = alpha * l_ref[h:h + 1, :]
        acc = alpha * acc_ref[h]
        for t in range(pp):
            l_new = l_new + jnp.sum(p[t], axis=1, keepdims=True)
            acc = acc + p[t] * v_refs[t][0, 0, h]
        m_ref[h:h + 1, :] = m_new
        l_ref[h:h + 1, :] = l_new
        acc_ref[h] = acc

    @pl.when(g == pl.num_programs(1) - 1)
    def _():
        col = jnp.sum(acc_ref[...], axis=2, keepdims=True).reshape(D_C, 1)
        acct_ref[...] = jnp.where(_iota((D_C, ns), 1) == b, col, acct_ref[...])
        ml = jnp.concatenate([m_ref[...], jnp.zeros((2 * SUBLANES - H_C, PAGE_SIZE), F32),
                              l_ref[...], jnp.zeros((2 * SUBLANES - H_C, PAGE_SIZE), F32)], axis=0)
        mlt_ref[...] = jnp.where(_iota((ML_ROWS, ns), 1) == b, ml, mlt_ref[...])


def _paged_attn(pt_flat, q_t, kt_pool, vt_pool, suf_pool, offsets, layer, ns, n_pages, *, pp=8):
    ng = n_pages // pp

    def page_map(t):
        return lambda b, g, pt: (layer, pt[b * n_pages + g * pp + t], 0, 0, 0)

    def suf_map(t):
        return lambda b, g, pt: (layer, 0, pt[b * n_pages + g * pp + t], 0, 0)

    kv_specs = [pl.BlockSpec((1, 1, H_C, HEAD_DIM, PAGE_SIZE), page_map(t)) for t in range(pp)]
    suf_specs = [pl.BlockSpec((1, H_C, 1, 1, PAGE_SIZE), suf_map(t)) for t in range(pp)]
    grid_spec = pltpu.PrefetchScalarGridSpec(
        num_scalar_prefetch=1,
        grid=(ns, ng),
        in_specs=[pl.BlockSpec((D_C, ns), lambda b, g, pt: (0, 0))] + kv_specs + kv_specs + suf_specs
        + [pl.BlockSpec((1, pp, H_C, PAGE_SIZE), lambda b, g, pt: (b, g, 0, 0))],
        out_specs=[pl.BlockSpec((D_C, ns), lambda b, g, pt: (0, 0)),
                   pl.BlockSpec((ML_ROWS, ns), lambda b, g, pt: (0, 0))],
        scratch_shapes=[
            pltpu.VMEM((H_C, HEAD_DIM, PAGE_SIZE), F32),
            pltpu.VMEM((H_C, PAGE_SIZE), F32),
            pltpu.VMEM((H_C, PAGE_SIZE), F32),
            pltpu.VMEM((H_C, HEAD_DIM, PAGE_SIZE), F32),
        ],
    )
    return pl.pallas_call(
        functools.partial(_paged_body, pp=pp, ns=ns),
        grid_spec=grid_spec,
        out_shape=[jax.ShapeDtypeStruct((D_C, ns), F32), jax.ShapeDtypeStruct((ML_ROWS, ns), F32)],
        compiler_params=_cparams("arbitrary", "arbitrary"),
        name="paged_attn",
    )(pt_flat, q_t, *([kt_pool] * pp), *([vt_pool] * pp), *([suf_pool] * pp), offsets)


def _sample_attn_post_body(acct_ref, mlt_ref, q_ref, k_ref, v_ref, y_in_ref, o_ref, *, ns):
    acc = acct_ref[...].T
    ml = jnp.concatenate([mlt_ref[...], jnp.zeros((LANES - ML_ROWS, ns), F32)], axis=0).T
    m = ml[:, 0:LANES]
    l = pltpu.roll(ml, LANES - 2 * SUBLANES, axis=1)
    seg = _seg_matrix(D_C, LANES, HEAD_DIM)
    expand = _seg_matrix(D_C, LANES, HEAD_DIM, transpose=True)
    s_self = _dot_sel_r(q_ref[...] * k_ref[...], seg)
    m_all = jnp.maximum(m, s_self)
    w_past = jnp.exp(m - m_all)
    w_self = jnp.exp(s_self - m_all)
    denom = l * w_past + w_self
    o_ref[...] = (acc * _dot_sel_r(w_past, expand) + v_ref[...] * _dot_sel_r(w_self, expand)) \
        / _dot_sel_r(denom, expand)


def _sample_attn_post(acc_t, ml_t, q, k, v, yc_all, row0):
    ns = q.shape[0]
    full = lambda r, c: pl.BlockSpec((r, c), lambda i: (0, 0))
    rows_s = pl.BlockSpec((ns, D_C), lambda i: (row0 // ns, 0))
    return pl.pallas_call(
        functools.partial(_sample_attn_post_body, ns=ns),
        grid=(1,),
        in_specs=[full(D_C, ns), full(ML_ROWS, ns), full(ns, D_C), full(ns, D_C), full(ns, D_C), rows_s],
        out_specs=rows_s,
        out_shape=jax.ShapeDtypeStruct(yc_all.shape, F32),
        input_output_aliases={5: 0},
        compiler_params=_cparams("arbitrary"),
        name="sample_attn_post",
    )(acc_t, ml_t, q, k, v, yc_all)


def _route(lg):
    lane = _iota(lg.shape, 1)
    big = jnp.int32(1 << 20)
    gl = jnp.where(lane < N_GROUPS, lg, NEG_BIG)
    gmax = jnp.max(gl, axis=1, keepdims=True)
    g_sel = jnp.min(jnp.where(gl == gmax, lane, big), axis=1, keepdims=True)
    g_w = 1.0 / jnp.sum(jnp.where(lane < N_GROUPS, jnp.exp(gl - gmax), 0.0), axis=1, keepdims=True)
    in_group = (lane >= N_GROUPS) & (lane < N_GROUPS + N_EXPERTS) & \
        (jnp.right_shift(lane - N_GROUPS, 2) == g_sel)
    el = jnp.where(in_group, lg, NEG_BIG)
    e1 = jnp.max(el, axis=1, keepdims=True)
    i1 = jnp.min(jnp.where(el == e1, lane, big), axis=1, keepdims=True)
    el2 = jnp.where(lane == i1, NEG_BIG, el)
    e2 = jnp.max(el2, axis=1, keepdims=True)
    i2 = jnp.min(jnp.where(el2 == e2, lane, big), axis=1, keepdims=True)
    r = jnp.exp(e2 - e1)
    w1 = g_w / (1.0 + r)
    w2 = g_w * r / (1.0 + r)
    out = jnp.where(lane == 0, (i1 - N_GROUPS).astype(F32), 0.0)
    out = jnp.where(lane == 1, (i2 - N_GROUPS).astype(F32), out)
    out = jnp.where(lane == 2, w1, out)
    out = jnp.where(lane == 3, w2, out)
    return out


def _out_proj_body(x_ref, ya_ref, yb_ref, yc_ref, wah_ref, wbh_ref, wch_ref, wal_ref, wbl_ref, wcl_ref,
                   g_ref, wr_ref, rb_ref, x1_ref, h2_ref, route_ref, *, precise_tiles):
    precise = _is_in(pl.program_id(0), precise_tiles)
    ys = (ya_ref, yb_ref, yc_ref)
    whs = (wah_ref, wbh_ref, wch_ref)
    wls = (wal_ref, wbl_ref, wcl_ref)

    @pl.when(jnp.logical_not(precise))
    def _():
        d = _dot(ys[0][...].astype(BF16), whs[0][...])
        for y, wh in zip(ys[1:], whs[1:]):
            d = d + _dot(y[...].astype(BF16), wh[...])
        x1_ref[...] = x_ref[...] + d

    @pl.when(precise)
    def _():
        d = jnp.zeros(x1_ref.shape, F32)
        for y, wh, wl in zip(ys, whs, wls):
            yh, yl = _split2(y[...])
            d = d + _dot(yh, wh[...]) + _dot(yl, wh[...]) + _dot(yh, wl[...])
        x1_ref[...] = x_ref[...] + d

    x1 = x1_ref[...]
    ms = jnp.mean(x1 * x1, axis=-1, keepdims=True)
    h2 = x1 * lax.rsqrt(ms + EPS) * g_ref[...]
    h2_ref[...] = h2
    route_ref[...] = _route(_dg3(h2, wr_ref[...]) + rb_ref[...])


def _out_proj(x, ya, yb, yc, wo_hi, wo_lo, g, wr, rb, precise_tiles, *, tm):
    t = x.shape[0]
    row = lambda w: pl.BlockSpec((tm, w), lambda i: (i, 0))
    const = lambda r, c: _const_spec((r, c), 1)
    parts = lambda w: (w[0:D_A], w[D_A:D_A + D_B], w[D_A + D_B:])
    wspecs = [const(D_A, D_MODEL), const(D_B, D_MODEL), const(D_C, D_MODEL)]
    return pl.pallas_call(
        functools.partial(_out_proj_body, precise_tiles=precise_tiles),
        grid=(t // tm,),
        in_specs=[row(D_MODEL), row(D_A), row(D_B), row(D_C)] + wspecs + wspecs
        + [const(1, D_MODEL), const(D_MODEL, LANES), const(1, LANES)],
        out_specs=[row(D_MODEL), row(D_MODEL), row(LANES)],
        out_shape=[
            jax.ShapeDtypeStruct((t, D_MODEL), F32),
            jax.ShapeDtypeStruct((t, D_MODEL), F32),
            jax.ShapeDtypeStruct((t, LANES), F32),
        ],
        compiler_params=_cparams("parallel"),
        name="out_proj",
    )(x, ya, yb, yc, *parts(wo_hi), *parts(wo_lo), g, wr, rb)


MOE_TM = 256


def _moe_body(te_ref, nt_ref, xs_ref, wg_ref, wu_ref, wd_ref, ys_ref, wg16, wu16, wd16):
    i = pl.program_id(0)
    changed = te_ref[i] != te_ref[jnp.maximum(i - 1, 0)]

    @pl.when((i == 0) | changed)
    def _():
        wg16[...] = wg_ref[0].astype(BF16)
        wu16[...] = wu_ref[0].astype(BF16)
        wd16[...] = wd_ref[0].astype(BF16)

    @pl.when(i < nt_ref[0])
    def _():
        x = xs_ref[...].astype(BF16)
        act = _silu(_dot(x, wg16[...])) * _dot(x, wu16[...])
        ys_ref[...] = _dot(act.astype(BF16), wd16[...])

    @pl.when(i >= nt_ref[0])
    def _():
        ys_ref[...] = jnp.zeros(ys_ref.shape, F32)


def _moe_gemm(tile_expert, n_tiles, xs, wg, wu, wd, layer):
    p_rows = xs.shape[0]
    tm = MOE_TM
    grid_spec = pltpu.PrefetchScalarGridSpec(
        num_scalar_prefetch=2,
        grid=(p_rows // tm,),
        in_specs=[
            pl.BlockSpec((tm, D_MODEL), lambda i, te, nt: (i, 0)),
            pl.BlockSpec((None, 1, D_MODEL, D_FF), lambda i, te, nt: (layer, te[i], 0, 0)),
            pl.BlockSpec((None, 1, D_MODEL, D_FF), lambda i, te, nt: (layer, te[i], 0, 0)),
            pl.BlockSpec((None, 1, D_FF, D_MODEL), lambda i, te, nt: (layer, te[i], 0, 0)),
        ],
        out_specs=pl.BlockSpec((tm, D_MODEL), lambda i, te, nt: (i, 0)),
        scratch_shapes=[
            pltpu.VMEM((D_MODEL, D_FF), BF16),
            pltpu.VMEM((D_MODEL, D_FF), BF16),
            pltpu.VMEM((D_FF, D_MODEL), BF16),
        ],
    )
    return pl.pallas_call(
        _moe_body,
        grid_spec=grid_spec,
        out_shape=jax.ShapeDtypeStruct((p_rows, D_MODEL), F32),
        compiler_params=_cparams("arbitrary"),
        name="moe_gemm",
    )(tile_expert, n_tiles, xs, wg, wu, wd)


def _moe_plan(route, tm):
    t = route.shape[0]
    n_assign = 2 * t
    i32 = jnp.int32
    ids = route[:, 0:2].astype(i32).reshape(-1)
    experts = jnp.arange(N_EXPERTS, dtype=i32)[None, :]
    onehot = ids[:, None] == experts
    csum = jnp.cumsum(onehot.astype(i32), axis=0)
    counts = csum[-1]
    padded = ((counts + tm - 1) // tm) * tm
    pend = jnp.cumsum(padded)
    pstart = pend - padded
    start = jnp.cumsum(counts) - counts
    pick = lambda table, hot: jnp.sum(jnp.where(hot, table[None, :], 0), axis=1)
    pos = pick(pstart, onehot) + jnp.sum(jnp.where(onehot, csum, 0), axis=1) - 1
    p_rows = ((n_assign + N_EXPERTS * tm + tm - 1) // tm) * tm
    p = jnp.arange(p_rows, dtype=i32)
    e_p = jnp.minimum(jnp.sum(p[:, None] >= pend[None, :], axis=1), N_EXPERTS - 1).astype(i32)
    hot_p = e_p[:, None] == experts
    off = p - pick(pstart, hot_p)
    order = jnp.argsort(ids, stable=True).astype(i32)
    src = order[jnp.clip(pick(start, hot_p) + off, 0, n_assign - 1)]
    src_tok = jnp.where(off < pick(counts, hot_p), src // 2, 0)
    n_tiles = (pend[-1] // tm).astype(i32)
    tile_e = e_p[::tm]
    tile_e = jnp.minimum(tile_e, tile_e[jnp.maximum(n_tiles - 1, 0)])
    return src_tok, pos.reshape(t, 2), tile_e, n_tiles.reshape(1)


def _moe_dense3_body(x1_ref, g_ref, route_ref, wg_ref, wu_ref, wd_ref, x_in_ref, o_ref):
    e = pl.program_id(0)

    @pl.when(e == 0)
    def _():
        o_ref[...] = x1_ref[...]

    x1 = x1_ref[...]
    ms = jnp.mean(x1 * x1, axis=-1, keepdims=True)
    h2 = x1 * lax.rsqrt(ms + EPS) * g_ref[...]
    route = route_ref[...]
    ef = e.astype(F32)
    gate = jnp.where(route[:, 0:1] == ef, route[:, 2:3], 0.0) + jnp.where(route[:, 1:2] == ef, route[:, 3:4], 0.0)
    act = _silu(_dg3(h2, wg_ref[0])) * _dg3(h2, wu_ref[0])
    o_ref[...] += gate * _dg3(act, wd_ref[0])


def _moe_dense3(x1, row0, ns, g, route, wg, wu, wd, layer, x_all):
    rows = pl.BlockSpec((ns, D_MODEL), lambda e: (row0 // ns, 0))
    return pl.pallas_call(
        _moe_dense3_body,
        grid=(N_EXPERTS,),
        in_specs=[
            rows,
            pl.BlockSpec((1, D_MODEL), lambda e: (0, 0)),
            pl.BlockSpec((ns, LANES), lambda e: (row0 // ns, 0)),
            pl.BlockSpec((None, 1, D_MODEL, D_FF), lambda e: (layer, e, 0, 0)),
            pl.BlockSpec((None, 1, D_MODEL, D_FF), lambda e: (layer, e, 0, 0)),
            pl.BlockSpec((None, 1, D_FF, D_MODEL), lambda e: (layer, e, 0, 0)),
            rows,
        ],
        out_specs=rows,
        out_shape=jax.ShapeDtypeStruct(x_all.shape, F32),
        input_output_aliases={6: 0},
        compiler_params=_cparams("arbitrary"),
        name="moe_dense3",
    )(x1, g, route, wg, wu, wd, x_all)


def _pad_lanes(v, n=LANES):
    return jnp.pad(v, (0, n - v.shape[0]))[None, :]


def _block_diag(w):
    h, d, _ = w.shape
    eye = jnp.eye(h, dtype=w.dtype)
    return (eye[:, None, :, None] * w[:, :, None, :]).reshape(h * d, h * d)


def _hi_lo_body(w_ref, hi_ref, lo_ref):
    hi_ref[...], lo_ref[...] = _split2(w_ref[...])


def _hi_lo(w, *, tr=256):
    rows, cols = w.shape
    spec = pl.BlockSpec((tr, cols), lambda i: (i, 0))
    out = jax.ShapeDtypeStruct((rows, cols), BF16)
    return pl.pallas_call(
        _hi_lo_body,
        grid=(rows // tr,),
        in_specs=[spec],
        out_specs=[spec, spec],
        out_shape=[out, out],
        compiler_params=_cparams("parallel"),
        name="hi_lo_split",
    )(w)


def _pack_w_in(w):
    parts, start = [], 0
    for size in PROJ_SPLITS:
        parts.append(w[:, start:start + size])
        start += size
    xa, ga, z, xbc, dt, q, k, v, f = parts
    padc = lambda a: jnp.pad(a, ((0, 0), (0, LANES - a.shape[1])))
    return _hi_lo(jnp.concatenate([z, xbc, q, k, v, padc(dt), padc(f), xa, ga], axis=1))


def _pack_router(rg, rgb, re, reb):
    wr = jnp.concatenate([rg, jnp.transpose(re, (1, 0, 2)).reshape(D_MODEL, N_EXPERTS)], axis=1)
    wr = jnp.pad(wr, ((0, 0), (0, LANES - wr.shape[1])))
    rb = _pad_lanes(jnp.concatenate([rgb, reb.reshape(-1)]))
    return wr, rb


def kernel(x_prompt, x_sample, cache_k, cache_v, cache_logf, state_lru, state_lru_conv, state_ssm,
           state_ssm_conv, page_table, norm_mix, w_in, lru_conv_w, lru_conv_b, lru_w_a, lru_b_a, lru_w_x,
           lru_b_x, lru_lambda, ssm_conv_w, ssm_conv_b, ssm_dt_bias, ssm_a_log, ssm_d, ssm_norm,
           attn_q_norm, attn_k_norm, attn_f_bias, w_out, norm_ffn, router_group, router_group_bias,
           router_expert, router_expert_bias, moe_w_gate, moe_w_up, moe_w_down):
    nb, seq, _ = x_prompt.shape
    ns = x_sample.shape[0]
    depth = w_in.shape[0]
    n_pages = page_table.shape[1]
    tp = nb * seq
    tm_in, tm_out = 640, 320
    assert (tp + ns) % tm_in == 0 and (tp + ns) % tm_out == 0 and tp % ns == 0
    precise_in = _precise_tiles(tm_in, nb, seq, ns)
    precise_out = _precise_tiles(tm_out, nb, seq, ns)

    x = jnp.concatenate([x_prompt.reshape(tp, D_MODEL), x_sample.reshape(ns, D_MODEL)], axis=0)
    kt_pool = jnp.transpose(cache_k, (0, 1, 3, 4, 2))
    vt_pool = jnp.transpose(cache_v, (0, 1, 3, 4, 2))
    n_pool = cache_logf.shape[1]
    suf_rows, tot_rows = _logf_suffix(
        jnp.transpose(cache_logf, (0, 3, 1, 2)).reshape(depth * H_C * n_pool, PAGE_SIZE))
    suf_pool = suf_rows.reshape(depth, H_C, n_pool, 1, PAGE_SIZE)
    page_tot = tot_rows[:, 0].reshape(depth, H_C, n_pool)
    conv_lru = jnp.transpose(state_lru_conv, (0, 2, 1, 3))
    conv_ssm = jnp.transpose(state_ssm_conv, (0, 2, 1, 3))
    pt_flat = page_table.reshape(-1).astype(jnp.int32)

    outs_p, outs_s = [], []
    for l in range(depth):
        precise_tail = l < depth - 1
        wi_hi, wi_lo = _pack_w_in(w_in[l])
        u = _in_proj(x, norm_mix[l][None], wi_hi, wi_lo, precise_in, tm=tm_in)

        lcb, ba, bx, lam = lru_conv_b[l][None], lru_b_a[l][None], lru_b_x[l][None], lru_lambda[l][None]
        wa, wx = _block_diag(lru_w_a[l]), _block_diag(lru_w_x[l])
        scb = ssm_conv_b[l][None]
        dtb, alog = _pad_lanes(ssm_dt_bias[l]), _pad_lanes(ssm_a_log[l])
        ng = ssm_norm[l][None]
        gq = jnp.tile(attn_q_norm[l], H_C)[None]
        gk = jnp.tile(attn_k_norm[l], H_C)[None]
        fb = _pad_lanes(attn_f_bias[l])

        ya_p, lru_h_p, lru_conv_p = _lru_prompt(u, nb, seq, lru_conv_w[l], lcb, wa, ba, wx, bx, lam)
        yb_p, ssm_h_p, ssm_conv_p = _ssd_prompt(u, nb, seq, ssm_conv_w[l], scb, dtb, alog, _pad_lanes(ssm_d[l]), ng,
                                                precise_tail=precise_tail)
        k_p, v_p, lf_p, *attn_ops = _qk_prep_prompt(u, nb, seq, gq, gk, fb, with_lo=precise_tail)
        yc_p = _fox_prompt(attn_ops, nb, seq, tp + ns, precise_tail=precise_tail)

        (ya, lru_h_s, lru_conv_s, ssm_conv_s, dtxt, dat, bm, cm, xd, zg) = _sample_pre(
            u, tp, ns, state_lru[l], conv_lru[l], conv_ssm[l], lru_conv_w[l], lcb, wa, ba, wx, bx, lam,
            ssm_conv_w[l], scb, dtb, alog, jnp.repeat(ssm_d[l], HEAD_DIM)[None], ya_p)
        ssm_h_s, yb = _ssd_update(state_ssm, l, dtxt, dat, bm, cm, xd, zg, ng, yb_p, tp)
        q_s, k_s, lf_s = _qk_prep_sample(u, tp, ns, gq, gk, fb)
        v_s = u[tp:, OFF_V:OFF_V + D_C]
        tg = page_tot[l][:, page_table]
        later_pages = jnp.flip(jnp.cumsum(jnp.flip(tg, -1), -1), -1) - tg
        off = jnp.transpose(later_pages, (1, 2, 0)) + lf_s[:, None, 0:H_C]
        offsets = jnp.broadcast_to(off[..., None], (ns, n_pages, H_C, PAGE_SIZE))
        acc_t, ml_t = _paged_attn(pt_flat, q_s.T, kt_pool, vt_pool, suf_pool, offsets, l, ns, n_pages)
        yc = _sample_attn_post(acc_t, ml_t, q_s, k_s, v_s, yc_p, tp)
        wo_hi, wo_lo = _hi_lo(w_out[l])
        wr, rb = _pack_router(router_group[l], router_group_bias[l], router_expert[l], router_expert_bias[l])
        x1, h2, route = _out_proj(x, ya, yb, yc, wo_hi, wo_lo, norm_ffn[l][None], wr, rb, precise_out, tm=tm_out)

        src_tok, pos, tile_e, n_tiles = _moe_plan(route, MOE_TM)
        ys = _moe_gemm(tile_e, n_tiles, h2[src_tok], moe_w_gate, moe_w_up, moe_w_down, l)
        x = x1 + route[:, 2:3] * ys[pos[:, 0]] + route[:, 3:4] * ys[pos[:, 1]]
        if l < depth - 1:
            x = _moe_dense3(x1, tp, ns, norm_ffn[l][None], route, moe_w_gate, moe_w_up, moe_w_down, l, x)

        outs_p.append((
            k_p.reshape(nb, seq, H_C, HEAD_DIM), v_p.reshape(nb, seq, H_C, HEAD_DIM),
            lf_p[:, 0:H_C].reshape(nb, seq, H_C), lru_h_p[:, 0], lru_conv_p, ssm_h_p, ssm_conv_p))
        outs_s.append((
            k_s.reshape(ns, 1, H_C, HEAD_DIM), v_s.reshape(ns, 1, H_C, HEAD_DIM),
            lf_s[:, 0:H_C].reshape(ns, 1, H_C), lru_h_s, jnp.transpose(lru_conv_s, (1, 0, 2)),
            ssm_h_s, jnp.transpose(ssm_conv_s, (1, 0, 2))))

    stack = lambda states, j: jnp.stack([s[j] for s in states], axis=0)
    return (x[:tp].reshape(nb, seq, D_MODEL), x[tp:].reshape(ns, 1, D_MODEL),
            *[stack(outs_p, j) for j in range(7)], *[stack(outs_s, j) for j in range(7)])
```

```python
import functools
import math

import jax
import jax.numpy as jnp
from jax import lax
from jax.experimental import pallas as pl
from jax.experimental.pallas import tpu as pltpu

F32 = jnp.float32
BF16 = jnp.bfloat16

D_MODEL = 2048
HEAD_DIM = 64
D_A = 512
H_A = 8
D_B = 768
H_B = 12
G_B = 4
N_B = 128
CONV_B = D_B + 2 * G_B * N_B
D_C = 768
H_C = 12
CONV_W = 4
LRU_C = 8.0
SSD_CHUNK = 128
PAGE_SIZE = 128
ATTN_SCALE = 1.0 / math.sqrt(HEAD_DIM)
PROJ_SPLITS = (D_A, D_A, D_B, CONV_B, H_B, D_C, D_C, D_C, H_C)
N_GROUPS = 4
E_PER_GROUP = 4
N_EXPERTS = 16
D_FF = 512
EPS = 1e-6
NEG_BIG = -1e30

LANES = 128
SUBLANES = 8
VMEM_LIMIT = 56 * 1024 * 1024

TAIL = 256
LOG2E = math.log2(math.e)

OFF_Z = 0
OFF_XBC = 768
OFF_Q = 2560
OFF_K = 3328
OFF_V = 4096
OFF_DT = 4864
OFF_F = 4992
OFF_XA = 5120
OFF_GA = 5632
N_U = 6144


def _cparams(*sem):
    return pltpu.CompilerParams(dimension_semantics=sem, vmem_limit_bytes=VMEM_LIMIT)


_NN = (((1,), (0,)), ((), ()))
_NT = (((1,), (1,)), ((), ()))
_TN = (((0,), (0,)), ((), ()))


def _dg(a, b, dims=_NN):
    return lax.dot_general(a, b, dims, preferred_element_type=F32)


def _dot(a, b):
    return _dg(a, b, _NN)


def _split2(x):
    hi = x.astype(BF16)
    lo = (x - hi.astype(F32)).astype(BF16)
    return hi, lo


def _split3(x):
    hi = x.astype(BF16)
    r = x - hi.astype(F32)
    mid = r.astype(BF16)
    lo = (r - mid.astype(F32)).astype(BF16)
    return hi, mid, lo


def _dot_sel_r(x, sel):
    hi, mid, lo = _split3(x)
    return _dot(hi, sel) + _dot(mid, sel) + _dot(lo, sel)


def _dot_sel_l(sel, x):
    hi, mid, lo = _split3(x)
    return _dot(sel, hi) + _dot(sel, mid) + _dot(sel, lo)


def _dg3(a, b, dims=_NN):
    ah, al = _split2(a)
    bh, bl = _split2(b)
    return _dg(ah, bh, dims) + _dg(al, bh, dims) + _dg(ah, bl, dims)


def _mm(a, b, dims, precise):
    if precise:
        return _dg3(a, b, dims)
    return _dg(a.astype(BF16), b.astype(BF16), dims)


def _sigmoid(x):
    return 1.0 / (1.0 + jnp.exp(-x))


def _silu(x):
    return x * _sigmoid(x)


def _log_sigmoid(x):
    return jnp.minimum(x, 0.0) - jnp.log1p(jnp.exp(-jnp.abs(x)))


def _softplus(x):
    return jnp.maximum(x, 0.0) + jnp.log1p(jnp.exp(-jnp.abs(x)))


def _gelu_tanh(x):
    return 0.5 * x * (1.0 + jnp.tanh(math.sqrt(2.0 / math.pi) * (x + 0.044715 * (x * x * x))))


def _iota(shape, dim):
    return lax.broadcasted_iota(jnp.int32, shape, dim)


def _seg_matrix(n_rows, n_cols, seg, transpose=False):
    if transpose:
        m = (_iota((n_cols, n_rows), 1) // seg) == _iota((n_cols, n_rows), 0)
    else:
        m = (_iota((n_rows, n_cols), 0) // seg) == _iota((n_rows, n_cols), 1)
    return jnp.where(m, 1.0, 0.0).astype(BF16)


def _precise_tiles(tm, nb, seq, ns):
    spans = [((b + 1) * seq - TAIL, (b + 1) * seq) for b in range(nb)] + [(nb * seq, nb * seq + ns)]
    n_tiles = (nb * seq + ns) // tm
    return tuple(i for i in range(n_tiles) if any(lo < (i + 1) * tm and hi > i * tm for lo, hi in spans))


def _is_in(i, ids):
    hit = i == ids[0]
    for t in ids[1:]:
        hit = hit | (i == t)
    return hit


def _const_spec(shape, n_grid):
    zeros = (0,) * len(shape)
    return pl.BlockSpec(shape, lambda *_: zeros, pipeline_mode=pl.Buffered(1))


def _in_proj_body(x_ref, g_ref, wh_ref, wl_ref, o_ref, xh_ref, xl_ref, *, precise_tiles):
    precise = _is_in(pl.program_id(0), precise_tiles)

    @pl.when(pl.program_id(1) == 0)
    def _():
        x = x_ref[...]
        ms = jnp.mean(x * x, axis=-1, keepdims=True)
        xh_ref[...], xl_ref[...] = _split2(x * lax.rsqrt(ms + EPS) * g_ref[...])

    @pl.when(jnp.logical_not(precise))
    def _():
        o_ref[...] = _dot(xh_ref[...], wh_ref[...])

    @pl.when(precise)
    def _():
        o_ref[...] = (_dot(xh_ref[...], wh_ref[...]) + _dot(xl_ref[...], wh_ref[...])
                      + _dot(xh_ref[...], wl_ref[...]))


def _in_proj(x, g, wh, wl, precise_tiles, *, tm, tn=1024):
    t = x.shape[0]
    return pl.pallas_call(
        functools.partial(_in_proj_body, precise_tiles=precise_tiles),
        grid=(t // tm, N_U // tn),
        in_specs=[
            pl.BlockSpec((tm, D_MODEL), lambda i, j: (i, 0)),
            pl.BlockSpec((1, D_MODEL), lambda i, j: (0, 0)),
            pl.BlockSpec((D_MODEL, tn), lambda i, j: (0, j)),
            pl.BlockSpec((D_MODEL, tn), lambda i, j: (0, jnp.where(_is_in(i, precise_tiles), j, 0))),
        ],
        out_specs=pl.BlockSpec((tm, tn), lambda i, j: (i, j)),
        out_shape=jax.ShapeDtypeStruct((t, N_U), F32),
        scratch_shapes=[pltpu.VMEM((tm, D_MODEL), BF16), pltpu.VMEM((tm, D_MODEL), BF16)],
        compiler_params=_cparams("parallel", "arbitrary"),
        name="in_proj",
    )(x, g, wh, wl)


def _lru_gates(xc, wa, ba, wx, bx, lam):
    r = _sigmoid(_dg3(xc, wa) + ba)
    i = _sigmoid(_dg3(xc, wx) + bx)
    log_a = LRU_C * r * _log_sigmoid(lam)
    a = jnp.exp(log_a)
    th = jnp.tanh(log_a)
    mult = jnp.sqrt(-2.0 * th / (1.0 - th))
    return a, mult * (i * xc)


def _lru_prompt_body(u_ref, cw_ref, cb_ref, wa_ref, ba_ref, wx_ref, bx_ref, lam_ref,
                     y_ref, h_out_ref, conv_out_ref, xp_ref, h_ref, *, tl):
    i = pl.program_id(1)

    @pl.when(i == 0)
    def _():
        xp_ref[0:SUBLANES, :] = jnp.zeros((SUBLANES, D_A), F32)
        h_ref[...] = jnp.zeros((1, D_A), F32)

    x = u_ref[:, 0:D_A]
    ga = u_ref[:, D_A:2 * D_A]
    xp_ref[SUBLANES:SUBLANES + tl, :] = x
    cw = cw_ref[...]
    xc = cb_ref[...] + cw[3:4, :] * x
    for j in range(CONV_W - 1):
        xc = xc + cw[j:j + 1, :] * xp_ref[pl.ds(SUBLANES - 3 + j, tl), :]
    xp_ref[0:SUBLANES, :] = x[tl - SUBLANES:tl, :]

    a, u = _lru_gates(xc, wa_ref[...], ba_ref[...], wx_ref[...], bx_ref[...], lam_ref[...])

    row = _iota((tl, D_A), 0)
    s = 1
    while s < tl:
        keep = row >= s
        a_sh = jnp.where(keep, pltpu.roll(a, s, axis=0), 1.0)
        u_sh = jnp.where(keep, pltpu.roll(u, s, axis=0), 0.0)
        u = u + a * u_sh
        a = a * a_sh
        s *= 2
    h = a * h_ref[...] + u
    h_ref[...] = h[tl - 1:tl, :]
    y_ref[...] = h * _gelu_tanh(ga)

    @pl.when(i == pl.num_programs(1) - 1)
    def _():
        h_out_ref[0] = h[tl - 1:tl, :]
        conv_out_ref[0] = x[tl - 3:tl, :]


def _lru_prompt(u, nb, seq, cw, cb, wa, ba, wx, bx, lam, *, tl=512):
    nt = seq // tl
    vec = lambda n: pl.BlockSpec((1, n), lambda b, i: (0, 0))
    mat = lambda r, c: pl.BlockSpec((r, c), lambda b, i: (0, 0))
    rb = lambda b, i: b * nt + i
    return pl.pallas_call(
        functools.partial(_lru_prompt_body, tl=tl),
        grid=(nb, nt),
        in_specs=[
            pl.BlockSpec((tl, 2 * D_A), lambda b, i: (rb(b, i), OFF_XA // (2 * D_A))),
            mat(CONV_W, D_A), vec(D_A), mat(D_A, D_A), vec(D_A), mat(D_A, D_A), vec(D_A), vec(D_A),
        ],
        out_specs=[
            pl.BlockSpec((tl, D_A), lambda b, i: (rb(b, i), 0)),
            pl.BlockSpec((1, 1, D_A), lambda b, i: (b, 0, 0)),
            pl.BlockSpec((1, CONV_W - 1, D_A), lambda b, i: (b, 0, 0)),
        ],
        out_shape=[
            jax.ShapeDtypeStruct((u.shape[0], D_A), F32),
            jax.ShapeDtypeStruct((nb, 1, D_A), F32),
            jax.ShapeDtypeStruct((nb, CONV_W - 1, D_A), F32),
        ],
        scratch_shapes=[pltpu.VMEM((SUBLANES + tl, D_A), F32), pltpu.VMEM((1, D_A), F32)],
        compiler_params=_cparams("parallel", "arbitrary"),
        name="lru_prompt",
    )(u, cw, cb, wa, ba, wx, bx, lam)


def _group_rmsnorm(y, g):
    gw = D_B // G_B
    ss = _dot_sel_r(y * y, _seg_matrix(D_B, LANES, gw))
    rs = lax.rsqrt(ss * (1.0 / gw) + EPS)
    return y * _dot_sel_r(rs, _seg_matrix(D_B, LANES, gw, transpose=True)) * g


def _ssd_chunk(x, bm, cm, dt, dta, dsk, h_ref, precise):
    q = SSD_CHUNK
    tri = jnp.where(_iota((q, q), 0) >= _iota((q, q), 1), 1.0, 0.0).astype(BF16)
    cum = _dot_sel_l(tri, dta)
    cum_t = cum.T
    dt_t = dt.T
    causal = _iota((q, q), 0) >= _iota((q, q), 1)
    ys = []
    for g in range(G_B):
        bg = bm[:, g * N_B:(g + 1) * N_B]
        cg = cm[:, g * N_B:(g + 1) * N_B]
        scores = _mm(cg, bg, _NT, precise)
        for r in range(H_B // G_B):
            h = g * (H_B // G_B) + r
            xh = x[:, h * HEAD_DIM:(h + 1) * HEAD_DIM]
            cl = cum[:, h:h + 1]
            cs = cum_t[h:h + 1, :]
            c_last = cum_t[h:h + 1, q - 1:q]
            decay = jnp.exp(jnp.where(causal, cl - cs, NEG_BIG))
            w = scores * decay * dt_t[h:h + 1, :]
            y_diag = _mm(w, xh, _NN, precise)
            h_prev = h_ref[h]
            y_off = _mm(cg, h_prev, _NT, precise) * jnp.exp(cl)
            de = jnp.exp(c_last - cl) * dt[:, h:h + 1]
            st = _mm(xh * de, bg, _TN, precise)
            h_ref[h] = jnp.exp(c_last) * h_prev + st
            ys.append(y_diag + y_off + dsk[:, h:h + 1] * xh)
    return jnp.concatenate(ys, axis=1)


def _ssd_prompt_body(u_ref, s_ref, cw_ref, cb_ref, dtb_ref, alog_ref, dsk_ref, ng_ref,
                     y_ref, h_out_ref, conv_out_ref, xp_ref, h_ref, *, precise_tail):
    c = pl.program_id(1)
    nc = pl.num_programs(1)
    q = SSD_CHUNK

    @pl.when(c == 0)
    def _():
        xp_ref[0:SUBLANES, :] = jnp.zeros((SUBLANES, CONV_B), F32)
        h_ref[...] = jnp.zeros((H_B, HEAD_DIM, N_B), F32)

    z = u_ref[:, OFF_Z:OFF_Z + D_B]
    xbc = u_ref[:, OFF_XBC:OFF_XBC + CONV_B]
    xp_ref[SUBLANES:SUBLANES + q, :] = xbc
    cw = cw_ref[...]
    xc = cb_ref[...] + cw[3:4, :] * xbc
    for j in range(CONV_W - 1):
        xc = xc + cw[j:j + 1, :] * xp_ref[pl.ds(SUBLANES - 3 + j, q), :]
    xp_ref[0:SUBLANES, :] = xbc[q - SUBLANES:q, :]
    xc = _silu(xc)
    x = xc[:, 0:D_B]
    bm = xc[:, D_B:D_B + G_B * N_B]
    cm = xc[:, D_B + G_B * N_B:]
    dt = _softplus(s_ref[:, 0:LANES] + dtb_ref[...])
    dta = dt * (-jnp.exp(alog_ref[...]))
    gate = _silu(z)

    def finish(precise):
        y = _ssd_chunk(x, bm, cm, dt, dta, dsk_ref[...], h_ref, precise) * gate
        y_ref[...] = _group_rmsnorm(y, ng_ref[...])

    if precise_tail:
        in_tail = c >= nc - TAIL // q
        pl.when(in_tail)(lambda: finish(True))
        pl.when(jnp.logical_not(in_tail))(lambda: finish(False))
    else:
        finish(False)

    @pl.when(c == nc - 1)
    def _():
        h_out_ref[0] = h_ref[...]
        conv_out_ref[0] = xbc[q - 3:q, :]


def _ssd_prompt(u, nb, seq, cw, cb, dtb, alog, dsk, ng, *, precise_tail):
    nc = seq // SSD_CHUNK
    q = SSD_CHUNK
    vec = lambda n: pl.BlockSpec((1, n), lambda b, c: (0, 0))
    rb = lambda b, c: b * nc + c
    return pl.pallas_call(
        functools.partial(_ssd_prompt_body, precise_tail=precise_tail),
        grid=(nb, nc),
        in_specs=[
            pl.BlockSpec((q, OFF_Q), lambda b, c: (rb(b, c), 0)),
            pl.BlockSpec((q, 2 * LANES), lambda b, c: (rb(b, c), OFF_DT // (2 * LANES))),
            pl.BlockSpec((CONV_W, CONV_B), lambda b, c: (0, 0)),
            vec(CONV_B), vec(LANES), vec(LANES), vec(LANES), vec(D_B),
        ],
        out_specs=[
            pl.BlockSpec((q, D_B), lambda b, c: (rb(b, c), 0)),
            pl.BlockSpec((1, H_B, HEAD_DIM, N_B), lambda b, c: (b, 0, 0, 0)),
            pl.BlockSpec((1, CONV_W - 1, CONV_B), lambda b, c: (b, 0, 0)),
        ],
        out_shape=[
            jax.ShapeDtypeStruct((u.shape[0], D_B), F32),
            jax.ShapeDtypeStruct((nb, H_B, HEAD_DIM, N_B), F32),
            jax.ShapeDtypeStruct((nb, CONV_W - 1, CONV_B), F32),
        ],
        scratch_shapes=[pltpu.VMEM((SUBLANES + q, CONV_B), F32),
                        pltpu.VMEM((H_B, HEAD_DIM, N_B), F32)],
        compiler_params=_cparams("parallel", "arbitrary"),
        name="ssd_prompt",
    )(u, u, cw, cb, dtb, alog, dsk, ng)


def _head_rmsnorm(x, g):
    ss = _dot_sel_r(x * x, _seg_matrix(D_C, LANES, HEAD_DIM))
    rs = lax.rsqrt(ss * (1.0 / HEAD_DIM) + EPS)
    return x * _dot_sel_r(rs, _seg_matrix(D_C, LANES, HEAD_DIM, transpose=True)) * g


D_CX = H_C * LANES


def _widen_heads(x, extras):
    parts = []
    for h in range(H_C):
        parts += [x[:, h * HEAD_DIM:(h + 1) * HEAD_DIM], extras[h]]
    return jnp.concatenate(parts, axis=1)


def _qk_prep_prompt_body(u_ref, gq_ref, gk_ref, fb_ref, k_ref, v_ref, lf_ref, qh_ref, kh_ref, vh_ref,
                         *rest, tl, with_lo):
    if with_lo:
        ql_ref, kl_ref, vl_ref, carry_ref = rest
    else:
        (carry_ref,) = rest
    q = u_ref[:, 0:D_C]
    k = u_ref[:, D_C:2 * D_C]
    v = u_ref[:, 2 * D_C:3 * D_C]
    f_raw = u_ref[:, OFF_F - OFF_Q:OFF_F - OFF_Q + LANES]
    qn = _head_rmsnorm(q, gq_ref[...]) * (ATTN_SCALE * LOG2E)
    kn = _head_rmsnorm(k, gk_ref[...])
    k_ref[...] = kn
    v_ref[...] = v
    lf = _log_sigmoid(f_raw + fb_ref[...])
    lf_ref[...] = lf

    @pl.when(pl.program_id(1) == 0)
    def _():
        carry_ref[...] = jnp.zeros((1, LANES), F32)

    tri = jnp.where(_iota((tl, tl), 0) >= _iota((tl, tl), 1), 1.0, 0.0).astype(BF16)
    c = _dot_sel_l(tri, lf) + carry_ref[...]
    carry_ref[...] = c[tl - 1:tl, :]
    b3 = [p.astype(F32) for p in _split3(c * (-LOG2E))]

    lane = _iota((tl, HEAD_DIM), 1)
    zeros = jnp.zeros((tl, HEAD_DIM), F32)
    ones = jnp.ones((tl, HEAD_DIM), F32)
    sel3 = jnp.where(lane < 3, 1.0, 0.0)
    bias = [jnp.where(lane == 0, b3[0][:, h:h + 1],
                      jnp.where(lane == 1, b3[1][:, h:h + 1],
                                jnp.where(lane == 2, b3[2][:, h:h + 1], 0.0))) for h in range(H_C)]

    def hi_lo(x):
        hi = x.astype(BF16)
        return hi, x - hi.astype(F32)

    q_hi, q_rem = hi_lo(qn)
    k_hi, k_rem = hi_lo(kn)
    v_hi, v_rem = hi_lo(v)
    qh_ref[...] = _widen_heads(q_hi.astype(F32), [sel3] * H_C).astype(BF16)
    kh_ref[...] = _widen_heads(k_hi.astype(F32), bias).astype(BF16)
    vh_ref[0] = _widen_heads(v_hi.astype(F32), [ones] * H_C).T.astype(BF16)
    if with_lo:
        ql_ref[...] = _widen_heads(q_rem, [zeros] * H_C).astype(BF16)
        kl_ref[...] = _widen_heads(k_rem, [zeros] * H_C).astype(BF16)
        vl_ref[0] = _widen_heads(v_rem, [zeros] * H_C).T.astype(BF16)


def _qk_prep_prompt(u, nb, seq, gq, gk, fb, *, with_lo, tl=512):
    nt = seq // tl
    rows = nb * seq
    wq = OFF_XA - OFF_Q
    vec = lambda n: pl.BlockSpec((1, n), lambda b, i: (0, 0))
    nat = lambda w: pl.BlockSpec((tl, w), lambda b, i: (b * nt + i, 0))
    sx = jax.ShapeDtypeStruct((rows, D_CX), BF16)
    s32 = jax.ShapeDtypeStruct((rows, D_C), F32)
    vt_spec = pl.BlockSpec((1, D_CX, tl), lambda b, i: (b, 0, i))
    sxt = jax.ShapeDtypeStruct((nb, D_CX, seq), BF16)
    reps = 2 if with_lo else 1
    return pl.pallas_call(
        functools.partial(_qk_prep_prompt_body, tl=tl, with_lo=with_lo),
        grid=(nb, nt),
        in_specs=[
            pl.BlockSpec((tl, wq), lambda b, i: (b * nt + i, OFF_Q // wq)),
            vec(D_C), vec(D_C), vec(LANES),
        ],
        out_specs=[nat(D_C), nat(D_C), nat(LANES)] + [nat(D_CX), nat(D_CX), vt_spec] * reps,
        out_shape=[s32, s32, jax.ShapeDtypeStruct((rows, LANES), F32)] + [sx, sx, sxt] * reps,
        scratch_shapes=[pltpu.VMEM((1, LANES), F32)],
        compiler_params=_cparams("parallel", "arbitrary"),
        name="qk_prep_prompt",
    )(u, gq, gk, fb)


def _qk_prep_sample_body(u_ref, gq_ref, gk_ref, fb_ref, q_ref, k_ref, lf_ref):
    q = u_ref[:, 0:D_C]
    k = u_ref[:, D_C:2 * D_C]
    f_raw = u_ref[:, OFF_F - OFF_Q:OFF_F - OFF_Q + LANES]
    q_ref[...] = _head_rmsnorm(q, gq_ref[...]) * ATTN_SCALE
    k_ref[...] = _head_rmsnorm(k, gk_ref[...])
    lf_ref[...] = _log_sigmoid(f_raw + fb_ref[...])


def _qk_prep_sample(u, row0, ns, gq, gk, fb):
    wq = OFF_XA - OFF_Q
    full = lambda r, c: pl.BlockSpec((r, c), lambda i: (0, 0))
    return pl.pallas_call(
        _qk_prep_sample_body,
        grid=(1,),
        in_specs=[pl.BlockSpec((ns, wq), lambda i: (row0 // ns, OFF_Q // wq)),
                  full(1, D_C), full(1, D_C), full(1, LANES)],
        out_specs=[full(ns, D_C), full(ns, D_C), full(ns, LANES)],
        out_shape=[jax.ShapeDtypeStruct((ns, D_C), F32), jax.ShapeDtypeStruct((ns, D_C), F32),
                   jax.ShapeDtypeStruct((ns, LANES), F32)],
        compiler_params=_cparams("arbitrary"),
        name="qk_prep_sample",
    )(u, gq, gk, fb)


def _fox_tile(i, refs, tq, precise):
    if precise:
        qh_ref, kh_ref, vh_ref, ql_ref, kl_ref, vl_ref, o_ref = refs
    else:
        qh_ref, kh_ref, vh_ref, o_ref = refs
    causal_t = _iota((tq, tq), 0) <= _iota((tq, tq), 1)
    heads = [slice(hh * LANES, (hh + 1) * LANES) for hh in range(2)]
    qh = [qh_ref[:, sl] for sl in heads]
    ql = [ql_ref[:, sl] for sl in heads] if precise else None

    def scores(j, hh):
        off = pl.multiple_of(j * tq, tq)
        kh = kh_ref[pl.ds(off, tq), heads[hh]]
        s = _dg(kh, qh[hh], _NT)
        if precise:
            s = s + _dg(kh, ql[hh], _NT) + _dg(kl_ref[pl.ds(off, tq), heads[hh]], qh[hh], _NT)
        return s

    def update(j, hh, state, s, masked):
        m, acc = state
        off = pl.multiple_of(j * tq, tq)
        vh = vh_ref[0, heads[hh], pl.ds(off, tq)]
        if masked:
            s = jnp.where(causal_t, s, NEG_BIG)
        m_new = jnp.maximum(m, jnp.max(s, axis=0, keepdims=True))
        alpha = jnp.exp2(m - m_new)
        p = jnp.exp2(s - m_new)
        if precise:
            p_hi, p_lo = _split2(p)
            pv = _dot(vh, p_hi) + _dot(vh, p_lo) + _dot(vl_ref[0, heads[hh], pl.ds(off, tq)], p_hi)
        else:
            pv = _dot(vh, p.astype(BF16))
        return m_new, alpha * acc + pv

    def step(j, carry):
        out = []
        for hh in range(2):
            state, s_cur = carry[hh]
            s_next = scores(j + 1, hh)
            out.append((update(j, hh, state, s_cur, False), s_next))
        return tuple(out)

    init = tuple(((jnp.full((1, tq), NEG_BIG, F32), jnp.zeros((LANES, tq), F32)), scores(0, hh))
                 for hh in range(2))
    carry = lax.fori_loop(0, i, step, init)
    carry = tuple(update(i, hh, state, s_cur, True) for hh, (state, s_cur) in enumerate(carry))
    o_t = jnp.concatenate([acc[0:HEAD_DIM] / acc[HEAD_DIM:HEAD_DIM + 1] for _, acc in carry], axis=0)
    o_ref[...] = o_t.T


def _fox_body(*refs, tq, precise_tail):
    i = pl.program_id(2)
    if precise_tail:
        last = i == pl.num_programs(2) - 1
        plain = refs[0:3] + refs[6:]
        pl.when(last)(lambda: _fox_tile(i, refs, tq, True))
        pl.when(jnp.logical_not(last))(lambda: _fox_tile(i, plain, tq, False))
    else:
        _fox_tile(i, refs, tq, False)


def _fox_prompt(operands, nb, seq, total_rows, *, precise_tail, tq=512):
    nq = seq // tq
    npair = H_C // 2
    qspec = pl.BlockSpec((tq, 2 * LANES), lambda b, p, i: (b * nq + i, p))
    kspec = pl.BlockSpec((seq, 2 * LANES), lambda b, p, i: (b, p))
    vspec = pl.BlockSpec((1, 2 * LANES, seq), lambda b, p, i: (b, p, 0))
    return pl.pallas_call(
        functools.partial(_fox_body, tq=tq, precise_tail=precise_tail),
        grid=(nb, npair, nq),
        in_specs=[qspec, kspec, vspec] * (2 if precise_tail else 1),
        out_specs=pl.BlockSpec((tq, LANES), lambda b, p, i: (b * nq + i, p)),
        out_shape=jax.ShapeDtypeStruct((total_rows, D_C), F32),
        compiler_params=_cparams("parallel", "parallel", "arbitrary"),
        name="fox_prompt",
    )(*operands)


def _sample_pre_body(u_ref, hl_ref, cl_ref, cs_ref, lcw_ref, lcb_ref, wa_ref, ba_ref, wx_ref, bx_ref,
                     lam_ref, scw_ref, scb_ref, dtb_ref, alog_ref, dsk_ref, ya_in_ref,
                     ya_ref, hnew_ref, lconv_ref, sconv_ref, dtxt_ref, dat_ref, bm_ref, cm_ref,
                     xd_ref, zg_ref):
    xa = u_ref[:, OFF_XA:OFF_XA + D_A]
    ga = u_ref[:, OFF_GA:OFF_GA + D_A]
    cw = lcw_ref[...]
    xc = lcb_ref[...] + cw[3:4, :] * xa
    for j in range(CONV_W - 1):
        xc = xc + cw[j:j + 1, :] * cl_ref[j]
    lconv_ref[0] = cl_ref[1]
    lconv_ref[1] = cl_ref[2]
    lconv_ref[2] = xa
    a, uu = _lru_gates(xc, wa_ref[...], ba_ref[...], wx_ref[...], bx_ref[...], lam_ref[...])
    h = a * hl_ref[...] + uu
    hnew_ref[...] = h
    ya_ref[...] = h * _gelu_tanh(ga)

    z = u_ref[:, OFF_Z:OFF_Z + D_B]
    xbc = u_ref[:, OFF_XBC:OFF_XBC + CONV_B]
    cw = scw_ref[...]
    xs = scb_ref[...] + cw[3:4, :] * xbc
    for j in range(CONV_W - 1):
        xs = xs + cw[j:j + 1, :] * cs_ref[j]
    sconv_ref[0] = cs_ref[1]
    sconv_ref[1] = cs_ref[2]
    sconv_ref[2] = xbc
    xs = _silu(xs)
    x = xs[:, 0:D_B]
    bm_ref[...] = xs[:, D_B:D_B + G_B * N_B]
    cm_ref[...] = xs[:, D_B + G_B * N_B:]
    dt = _softplus(u_ref[:, OFF_DT:OFF_DT + LANES] + dtb_ref[...])
    da = jnp.exp(dt * (-jnp.exp(alog_ref[...])))
    expand = _seg_matrix(D_B, LANES, HEAD_DIM, transpose=True)
    dtxt_ref[...] = (_dot_sel_r(dt, expand) * x).T
    dat_ref[...] = _dot_sel_r(da, expand).T
    xd_ref[...] = dsk_ref[...] * x
    zg_ref[...] = _silu(z)


def _sample_pre(u, row0, ns, h_lru, conv_lru, conv_ssm, lcw, lcb, wa, ba, wx, bx, lam, scw, scb, dtb, alog, dsk_e,
                ya_all):
    full = lambda *shape: pl.BlockSpec(shape, lambda i: (0,) * len(shape))
    rows_s = lambda w: pl.BlockSpec((ns, w), lambda i: (row0 // ns, 0))
    f = lambda *shape: jax.ShapeDtypeStruct(shape, F32)
    return pl.pallas_call(
        _sample_pre_body,
        grid=(1,),
        in_specs=[
            pl.BlockSpec((ns, N_U), lambda i: (row0 // ns, 0)),
            full(ns, D_A), full(CONV_W - 1, ns, D_A), full(CONV_W - 1, ns, CONV_B),
            full(CONV_W, D_A), full(1, D_A), full(D_A, D_A), full(1, D_A), full(D_A, D_A), full(1, D_A),
            full(1, D_A), full(CONV_W, CONV_B), full(1, CONV_B), full(1, LANES), full(1, LANES), full(1, D_B),
            rows_s(D_A),
        ],
        out_specs=[
            rows_s(D_A), full(ns, D_A), full(CONV_W - 1, ns, D_A), full(CONV_W - 1, ns, CONV_B),
            full(D_B, ns), full(D_B, ns), full(ns, G_B * N_B), full(ns, G_B * N_B), full(ns, D_B), full(ns, D_B),
        ],
        out_shape=[
            f(*ya_all.shape), f(ns, D_A), f(CONV_W - 1, ns, D_A), f(CONV_W - 1, ns, CONV_B),
            f(D_B, ns), f(D_B, ns), f(ns, G_B * N_B), f(ns, G_B * N_B), f(ns, D_B), f(ns, D_B),
        ],
        input_output_aliases={16: 0},
        compiler_params=_cparams("arbitrary"),
        name="sample_pre",
    )(u, h_lru, conv_lru, conv_ssm, lcw, lcb, wa, ba, wx, bx, lam, scw, scb, dtb, alog, dsk_e, ya_all)


def _lane_bcast_column(parts, b, ns):
    onehot = jnp.where(_iota((ns, LANES), 0) == b, 1.0, 0.0).astype(BF16)
    out = _dot(parts[0], onehot)
    for p in parts[1:]:
        out = out + _dot(p, onehot)
    return out


def _ssd_update_body(st_ref, dtxt_ref, dat_ref, bm_ref, cm_ref, xd_ref, zg_ref, ng_ref, y_in_ref,
                     st_out_ref, y_ref, yt_ref, *, tb, ns):
    i = pl.program_id(0)

    @pl.when(i == 0)
    def _():
        yt_ref[...] = jnp.zeros((D_B, ns), F32)

    dtx3 = _split3(dtxt_ref[...])
    da3 = _split3(dat_ref[...])
    lane = _iota((D_B, ns), 1)
    rows_per_group = D_B // G_B

    def group_rows(row):
        return jnp.concatenate(
            [jnp.broadcast_to(row[:, g * N_B:(g + 1) * N_B], (rows_per_group, N_B)) for g in range(G_B)], axis=0)

    for t in range(tb):
        b = i * tb + t
        xb = _lane_bcast_column(dtx3, b, ns)
        dab = _lane_bcast_column(da3, b, ns)
        bexp = group_rows(bm_ref[pl.ds(b, 1), :])
        cexp = group_rows(cm_ref[pl.ds(b, 1), :])
        s_new = dab * st_ref[t].reshape(D_B, N_B) + xb * bexp
        st_out_ref[t] = s_new.reshape(H_B, HEAD_DIM, N_B)
        ycol = jnp.sum(s_new * cexp, axis=1, keepdims=True)
        yt_ref[...] = jnp.where(lane == b, ycol, yt_ref[...])

    @pl.when(i == pl.num_programs(0) - 1)
    def _():
        y = (yt_ref[...].T + xd_ref[...]) * zg_ref[...]
        y_ref[...] = _group_rmsnorm(y, ng_ref[...])


def _ssd_update(state, layer, dtxt, dat, bm, cm, xd, zg, ng, yb_all, row0, *, tb=8):
    ns = state.shape[1]
    full = lambda *shape: pl.BlockSpec(shape, lambda i: (0,) * len(shape))
    rows_s = pl.BlockSpec((ns, D_B), lambda i: (row0 // ns, 0))
    return pl.pallas_call(
        functools.partial(_ssd_update_body, tb=tb, ns=ns),
        grid=(ns // tb,),
        in_specs=[
            pl.BlockSpec((None, tb, H_B, HEAD_DIM, N_B), lambda i: (layer, i, 0, 0, 0)),
            full(D_B, ns), full(D_B, ns), full(ns, G_B * N_B), full(ns, G_B * N_B),
            full(ns, D_B), full(ns, D_B), full(1, D_B), rows_s,
        ],
        out_specs=[
            pl.BlockSpec((tb, H_B, HEAD_DIM, N_B), lambda i: (i, 0, 0, 0)),
            rows_s,
        ],
        out_shape=[
            jax.ShapeDtypeStruct((ns, H_B, HEAD_DIM, N_B), F32),
            jax.ShapeDtypeStruct(yb_all.shape, F32),
        ],
        scratch_shapes=[pltpu.VMEM((D_B, ns), F32)],
        input_output_aliases={8: 1},
        compiler_params=_cparams("arbitrary"),
        name="ssd_update",
    )(state, dtxt, dat, bm, cm, xd, zg, ng, yb_all)


def _logf_suffix_body(lf_ref, suf_ref, tot_ref):
    later = jnp.where(_iota((PAGE_SIZE, PAGE_SIZE), 0) > _iota((PAGE_SIZE, PAGE_SIZE), 1), 1.0, 0.0).astype(BF16)
    lf = lf_ref[...]
    suf_ref[...] = _dot_sel_r(lf, later)
    tot_ref[...] = _dot_sel_r(lf, jnp.ones((PAGE_SIZE, PAGE_SIZE), BF16))


def _logf_suffix(lf_rows, *, tr=2048):
    rows = lf_rows.shape[0]
    spec = pl.BlockSpec((tr, PAGE_SIZE), lambda i: (i, 0))
    out = jax.ShapeDtypeStruct((rows, PAGE_SIZE), F32)
    return pl.pallas_call(
        _logf_suffix_body,
        grid=(rows // tr,),
        in_specs=[spec],
        out_specs=[spec, spec],
        out_shape=[out, out],
        compiler_params=_cparams("parallel"),
        name="logf_suffix",
    )(lf_rows)


ML_ROWS = 4 * SUBLANES


def _paged_body(pt_ref, qt_ref, *refs, pp, ns):
    k_refs = refs[0:pp]
    v_refs = refs[pp:2 * pp]
    suf_refs = refs[2 * pp:3 * pp]
    off_ref, acct_ref, mlt_ref, qb_ref, m_ref, l_ref, acc_ref = refs[3 * pp:]
    b = pl.program_id(0)
    g = pl.program_id(1)

    @pl.when(g == 0)
    def _():
        qb_ref[...] = _lane_bcast_column(_split3(qt_ref[...]), b, ns).reshape(H_C, HEAD_DIM, PAGE_SIZE)
        m_ref[...] = jnp.full((H_C, PAGE_SIZE), NEG_BIG, F32)
        l_ref[...] = jnp.zeros((H_C, PAGE_SIZE), F32)
        acc_ref[...] = jnp.zeros((H_C, HEAD_DIM, PAGE_SIZE), F32)

    @pl.when((b == 0) & (g == 0))
    def _():
        acct_ref[...] = jnp.zeros(acct_ref.shape, F32)
        mlt_ref[...] = jnp.zeros(mlt_ref.shape, F32)

    for h in range(H_C):
        qh = qb_ref[h]
        s = [jnp.sum(qh * k_refs[t][0, 0, h], axis=0, keepdims=True)
             + suf_refs[t][0, h, 0] + off_ref[0, t, h:h + 1, :] for t in range(pp)]
        m_old = m_ref[h:h + 1, :]
        m_new = m_old
        for st in s:
            m_new = jnp.maximum(m_new, jnp.max(st, axis=1, keepdims=True))
        alpha = jnp.exp(m_old - m_new)
        p = [jnp.exp(st - m_new) for st in s]
        l_new = alpha * l_ref[h:h + 1, :]
        acc = alpha * acc_ref[h]
        for t in range(pp):
            l_new = l_new + jnp.sum(p[t], axis=1, keepdims=True)
            acc = acc + p[t] * v_refs[t][0, 0, h]
        m_ref[h:h + 1, :] = m_new
        l_ref[h:h + 1, :] = l_new
        acc_ref[h] = acc

    @pl.when(g == pl.num_programs(1) - 1)
    def _():
        col = jnp.sum(acc_ref[...], axis=2, keepdims=True).reshape(D_C, 1)
        acct_ref[...] = jnp.where(_iota((D_C, ns), 1) == b, col, acct_ref[...])
        ml = jnp.concatenate([m_ref[...], jnp.zeros((2 * SUBLANES - H_C, PAGE_SIZE), F32),
                              l_ref[...], jnp.zeros((2 * SUBLANES - H_C, PAGE_SIZE), F32)], axis=0)
        mlt_ref[...] = jnp.where(_iota((ML_ROWS, ns), 1) == b, ml, mlt_ref[...])


def _paged_attn(pt_flat, q_t, kt_pool, vt_pool, suf_pool, offsets, layer, ns, n_pages, *, pp=16):
    ng = n_pages // pp

    def page_map(t):
        return lambda b, g, pt: (layer, pt[b * n_pages + g * pp + t], 0, 0, 0)

    def suf_map(t):
        return lambda b, g, pt: (layer, 0, pt[b * n_pages + g * pp + t], 0, 0)

    kv_specs = [pl.BlockSpec((1, 1, H_C, HEAD_DIM, PAGE_SIZE), page_map(t)) for t in range(pp)]
    suf_specs = [pl.BlockSpec((1, H_C, 1, 1, PAGE_SIZE), suf_map(t)) for t in range(pp)]
    grid_spec = pltpu.PrefetchScalarGridSpec(
        num_scalar_prefetch=1,
        grid=(ns, ng),
        in_specs=[pl.BlockSpec((D_C, ns), lambda b, g, pt: (0, 0))] + kv_specs + kv_specs + suf_specs
        + [pl.BlockSpec((1, pp, H_C, PAGE_SIZE), lambda b, g, pt: (b, g, 0, 0))],
        out_specs=[pl.BlockSpec((D_C, ns), lambda b, g, pt: (0, 0)),
                   pl.BlockSpec((ML_ROWS, ns), lambda b, g, pt: (0, 0))],
        scratch_shapes=[
            pltpu.VMEM((H_C, HEAD_DIM, PAGE_SIZE), F32),
            pltpu.VMEM((H_C, PAGE_SIZE), F32),
            pltpu.VMEM((H_C, PAGE_SIZE), F32),
            pltpu.VMEM((H_C, HEAD_DIM, PAGE_SIZE), F32),
        ],
    )
    return pl.pallas_call(
        functools.partial(_paged_body, pp=pp, ns=ns),
        grid_spec=grid_spec,
        out_shape=[jax.ShapeDtypeStruct((D_C, ns), F32), jax.ShapeDtypeStruct((ML_ROWS, ns), F32)],
        compiler_params=_cparams("arbitrary", "arbitrary"),
        name="paged_attn",
    )(pt_flat, q_t, *([kt_pool] * pp), *([vt_pool] * pp), *([suf_pool] * pp), offsets)


def _sample_attn_post_body(acct_ref, mlt_ref, q_ref, k_ref, v_ref, y_in_ref, o_ref, *, ns):
    acc = acct_ref[...].T
    ml = jnp.concatenate([mlt_ref[...], jnp.zeros((LANES - ML_ROWS, ns), F32)], axis=0).T
    m = ml[:, 0:LANES]
    l = pltpu.roll(ml, LANES - 2 * SUBLANES, axis=1)
    seg = _seg_matrix(D_C, LANES, HEAD_DIM)
    expand = _seg_matrix(D_C, LANES, HEAD_DIM, transpose=True)
    s_self = _dot_sel_r(q_ref[...] * k_ref[...], seg)
    m_all = jnp.maximum(m, s_self)
    w_past = jnp.exp(m - m_all)
    w_self = jnp.exp(s_self - m_all)
    denom = l * w_past + w_self
    o_ref[...] = (acc * _dot_sel_r(w_past, expand) + v_ref[...] * _dot_sel_r(w_self, expand)) \
        / _dot_sel_r(denom, expand)


def _sample_attn_post(acc_t, ml_t, q, k, v, yc_all, row0):
    ns = q.shape[0]
    full = lambda r, c: pl.BlockSpec((r, c), lambda i: (0, 0))
    rows_s = pl.BlockSpec((ns, D_C), lambda i: (row0 // ns, 0))
    return pl.pallas_call(
        functools.partial(_sample_attn_post_body, ns=ns),
        grid=(1,),
        in_specs=[full(D_C, ns), full(ML_ROWS, ns), full(ns, D_C), full(ns, D_C), full(ns, D_C), rows_s],
        out_specs=rows_s,
        out_shape=jax.ShapeDtypeStruct(yc_all.shape, F32),
        input_output_aliases={5: 0},
        compiler_params=_cparams("arbitrary"),
        name="sample_attn_post",
    )(acc_t, ml_t, q, k, v, yc_all)


def _route(lg):
    lane = _iota(lg.shape, 1)
    big = jnp.int32(1 << 20)
    gl = jnp.where(lane < N_GROUPS, lg, NEG_BIG)
    gmax = jnp.max(gl, axis=1, keepdims=True)
    g_sel = jnp.min(jnp.where(gl == gmax, lane, big), axis=1, keepdims=True)
    g_w = 1.0 / jnp.sum(jnp.where(lane < N_GROUPS, jnp.exp(gl - gmax), 0.0), axis=1, keepdims=True)
    in_group = (lane >= N_GROUPS) & (lane < N_GROUPS + N_EXPERTS) & \
        (jnp.right_shift(lane - N_GROUPS, 2) == g_sel)
    el = jnp.where(in_group, lg, NEG_BIG)
    e1 = jnp.max(el, axis=1, keepdims=True)
    i1 = jnp.min(jnp.where(el == e1, lane, big), axis=1, keepdims=True)
    el2 = jnp.where(lane == i1, NEG_BIG, el)
    e2 = jnp.max(el2, axis=1, keepdims=True)
    i2 = jnp.min(jnp.where(el2 == e2, lane, big), axis=1, keepdims=True)
    r = jnp.exp(e2 - e1)
    w1 = g_w / (1.0 + r)
    w2 = g_w * r / (1.0 + r)
    out = jnp.where(lane == 0, (i1 - N_GROUPS).astype(F32), 0.0)
    out = jnp.where(lane == 1, (i2 - N_GROUPS).astype(F32), out)
    out = jnp.where(lane == 2, w1, out)
    out = jnp.where(lane == 3, w2, out)
    return out


def _out_proj_body(x_ref, ya_ref, yb_ref, yc_ref, wah_ref, wbh_ref, wch_ref, wal_ref, wbl_ref, wcl_ref,
                   g_ref, wr_ref, rb_ref, x1_ref, h2_ref, route_ref, *, precise_tiles):
    precise = _is_in(pl.program_id(0), precise_tiles)
    ys = (ya_ref, yb_ref, yc_ref)
    whs = (wah_ref, wbh_ref, wch_ref)
    wls = (wal_ref, wbl_ref, wcl_ref)

    @pl.when(jnp.logical_not(precise))
    def _():
        d = _dot(ys[0][...].astype(BF16), whs[0][...])
        for y, wh in zip(ys[1:], whs[1:]):
            d = d + _dot(y[...].astype(BF16), wh[...])
        x1_ref[...] = x_ref[...] + d

    @pl.when(precise)
    def _():
        d = jnp.zeros(x1_ref.shape, F32)
        for y, wh, wl in zip(ys, whs, wls):
            yh, yl = _split2(y[...])
            d = d + _dot(yh, wh[...]) + _dot(yl, wh[...]) + _dot(yh, wl[...])
        x1_ref[...] = x_ref[...] + d

    x1 = x1_ref[...]
    ms = jnp.mean(x1 * x1, axis=-1, keepdims=True)
    h2 = x1 * lax.rsqrt(ms + EPS) * g_ref[...]
    h2_ref[...] = h2.astype(BF16)
    route_ref[...] = _route(_dg3(h2, wr_ref[...]) + rb_ref[...])


def _out_proj(x, ya, yb, yc, wo_hi, wo_lo, g, wr, rb, precise_tiles, *, tm):
    t = x.shape[0]
    row = lambda w: pl.BlockSpec((tm, w), lambda i: (i, 0))
    const = lambda r, c: _const_spec((r, c), 1)
    parts = lambda w: (w[0:D_A], w[D_A:D_A + D_B], w[D_A + D_B:])
    wspecs = [const(D_A, D_MODEL), const(D_B, D_MODEL), const(D_C, D_MODEL)]
    return pl.pallas_call(
        functools.partial(_out_proj_body, precise_tiles=precise_tiles),
        grid=(t // tm,),
        in_specs=[row(D_MODEL), row(D_A), row(D_B), row(D_C)] + wspecs + wspecs
        + [const(1, D_MODEL), const(D_MODEL, LANES), const(1, LANES)],
        out_specs=[row(D_MODEL), row(D_MODEL), row(LANES)],
        out_shape=[
            jax.ShapeDtypeStruct((t, D_MODEL), F32),
            jax.ShapeDtypeStruct((t, D_MODEL), BF16),
            jax.ShapeDtypeStruct((t, LANES), F32),
        ],
        compiler_params=_cparams("parallel"),
        name="out_proj",
    )(x, ya, yb, yc, *parts(wo_hi), *parts(wo_lo), g, wr, rb)


MOE_TM = 256


def _moe_body(te_ref, nt_ref, xs_ref, wg_ref, wu_ref, wd_ref, ys_ref, wg16, wu16, wd16):
    i = pl.program_id(0)
    changed = te_ref[i] != te_ref[jnp.maximum(i - 1, 0)]

    @pl.when((i == 0) | changed)
    def _():
        wg16[...] = wg_ref[0].astype(BF16)
        wu16[...] = wu_ref[0].astype(BF16)
        wd16[...] = wd_ref[0].astype(BF16)

    @pl.when(i < nt_ref[0])
    def _():
        x = xs_ref[...]
        act = _silu(_dot(x, wg16[...])) * _dot(x, wu16[...])
        ys_ref[...] = _dot(act.astype(BF16), wd16[...])

    @pl.when(i >= nt_ref[0])
    def _():
        ys_ref[...] = jnp.zeros(ys_ref.shape, F32)


def _moe_gemm(tile_expert, n_tiles, xs, wg, wu, wd, layer):
    p_rows = xs.shape[0]
    tm = MOE_TM
    grid_spec = pltpu.PrefetchScalarGridSpec(
        num_scalar_prefetch=2,
        grid=(p_rows // tm,),
        in_specs=[
            pl.BlockSpec((tm, D_MODEL), lambda i, te, nt: (i, 0)),
            pl.BlockSpec((None, 1, D_MODEL, D_FF), lambda i, te, nt: (layer, te[i], 0, 0)),
            pl.BlockSpec((None, 1, D_MODEL, D_FF), lambda i, te, nt: (layer, te[i], 0, 0)),
            pl.BlockSpec((None, 1, D_FF, D_MODEL), lambda i, te, nt: (layer, te[i], 0, 0)),
        ],
        out_specs=pl.BlockSpec((tm, D_MODEL), lambda i, te, nt: (i, 0)),
        scratch_shapes=[
            pltpu.VMEM((D_MODEL, D_FF), BF16),
            pltpu.VMEM((D_MODEL, D_FF), BF16),
            pltpu.VMEM((D_FF, D_MODEL), BF16),
        ],
    )
    return pl.pallas_call(
        _moe_body,
        grid_spec=grid_spec,
        out_shape=jax.ShapeDtypeStruct((p_rows, D_MODEL), F32),
        compiler_params=_cparams("arbitrary"),
        name="moe_gemm",
    )(tile_expert, n_tiles, xs, wg, wu, wd)


def _moe_plan(route, tm):
    t = route.shape[0]
    n_assign = 2 * t
    i32 = jnp.int32
    ids = route[:, 0:2].astype(i32).reshape(-1)
    experts = jnp.arange(N_EXPERTS, dtype=i32)[None, :]
    onehot = ids[:, None] == experts
    csum = jnp.cumsum(onehot.astype(i32), axis=0)
    counts = csum[-1]
    padded = ((counts + tm - 1) // tm) * tm
    pend = jnp.cumsum(padded)
    pstart = pend - padded
    start = jnp.cumsum(counts) - counts
    pick = lambda table, hot: jnp.sum(jnp.where(hot, table[None, :], 0), axis=1)
    pos = pick(pstart, onehot) + jnp.sum(jnp.where(onehot, csum, 0), axis=1) - 1
    p_rows = ((n_assign + N_EXPERTS * tm + tm - 1) // tm) * tm
    p = jnp.arange(p_rows, dtype=i32)
    e_p = jnp.minimum(jnp.sum(p[:, None] >= pend[None, :], axis=1), N_EXPERTS - 1).astype(i32)
    hot_p = e_p[:, None] == experts
    off = p - pick(pstart, hot_p)
    order = jnp.argsort(ids, stable=True).astype(i32)
    src = order[jnp.clip(pick(start, hot_p) + off, 0, n_assign - 1)]
    src_tok = jnp.where(off < pick(counts, hot_p), src // 2, 0)
    n_tiles = (pend[-1] // tm).astype(i32)
    tile_e = e_p[::tm]
    tile_e = jnp.minimum(tile_e, tile_e[jnp.maximum(n_tiles - 1, 0)])
    return src_tok, pos.reshape(t, 2), tile_e, n_tiles.reshape(1)


def _moe_dense3_body(x1_ref, g_ref, route_ref, wg_ref, wu_ref, wd_ref, x_in_ref, o_ref):
    e = pl.program_id(0)

    @pl.when(e == 0)
    def _():
        o_ref[...] = x1_ref[...]

    x1 = x1_ref[...]
    ms = jnp.mean(x1 * x1, axis=-1, keepdims=True)
    h2 = x1 * lax.rsqrt(ms + EPS) * g_ref[...]
    route = route_ref[...]
    ef = e.astype(F32)
    gate = jnp.where(route[:, 0:1] == ef, route[:, 2:3], 0.0) + jnp.where(route[:, 1:2] == ef, route[:, 3:4], 0.0)
    act = _silu(_dg3(h2, wg_ref[0])) * _dg3(h2, wu_ref[0])
    o_ref[...] += gate * _dg3(act, wd_ref[0])


def _moe_dense3(x1, row0, ns, g, route, wg, wu, wd, layer, x_all):
    rows = pl.BlockSpec((ns, D_MODEL), lambda e: (row0 // ns, 0))
    return pl.pallas_call(
        _moe_dense3_body,
        grid=(N_EXPERTS,),
        in_specs=[
            rows,
            pl.BlockSpec((1, D_MODEL), lambda e: (0, 0)),
            pl.BlockSpec((ns, LANES), lambda e: (row0 // ns, 0)),
            pl.BlockSpec((None, 1, D_MODEL, D_FF), lambda e: (layer, e, 0, 0)),
            pl.BlockSpec((None, 1, D_MODEL, D_FF), lambda e: (layer, e, 0, 0)),
            pl.BlockSpec((None, 1, D_FF, D_MODEL), lambda e: (layer, e, 0, 0)),
            rows,
        ],
        out_specs=rows,
        out_shape=jax.ShapeDtypeStruct(x_all.shape, F32),
        input_output_aliases={6: 0},
        compiler_params=_cparams("arbitrary"),
        name="moe_dense3",
    )(x1, g, route, wg, wu, wd, x_all)


def _pad_lanes(v, n=LANES):
    return jnp.pad(v, (0, n - v.shape[0]))[None, :]


def _block_diag(w):
    h, d, _ = w.shape
    eye = jnp.eye(h, dtype=w.dtype)
    return (eye[:, None, :, None] * w[:, :, None, :]).reshape(h * d, h * d)


def _hi_lo_body(w_ref, hi_ref, lo_ref):
    hi_ref[...], lo_ref[...] = _split2(w_ref[...])


def _hi_lo(w, *, tr=256):
    rows, cols = w.shape
    spec = pl.BlockSpec((tr, cols), lambda i: (i, 0))
    out = jax.ShapeDtypeStruct((rows, cols), BF16)
    return pl.pallas_call(
        _hi_lo_body,
        grid=(rows // tr,),
        in_specs=[spec],
        out_specs=[spec, spec],
        out_shape=[out, out],
        compiler_params=_cparams("parallel"),
        name="hi_lo_split",
    )(w)


def _pack_w_in(w):
    parts, start = [], 0
    for size in PROJ_SPLITS:
        parts.append(w[:, start:start + size])
        start += size
    xa, ga, z, xbc, dt, q, k, v, f = parts
    padc = lambda a: jnp.pad(a, ((0, 0), (0, LANES - a.shape[1])))
    return _hi_lo(jnp.concatenate([z, xbc, q, k, v, padc(dt), padc(f), xa, ga], axis=1))


def _pack_router(rg, rgb, re, reb):
    wr = jnp.concatenate([rg, jnp.transpose(re, (1, 0, 2)).reshape(D_MODEL, N_EXPERTS)], axis=1)
    wr = jnp.pad(wr, ((0, 0), (0, LANES - wr.shape[1])))
    rb = _pad_lanes(jnp.concatenate([rgb, reb.reshape(-1)]))
    return wr, rb


def kernel(x_prompt, x_sample, cache_k, cache_v, cache_logf, state_lru, state_lru_conv, state_ssm,
           state_ssm_conv, page_table, norm_mix, w_in, lru_conv_w, lru_conv_b, lru_w_a, lru_b_a, lru_w_x,
           lru_b_x, lru_lambda, ssm_conv_w, ssm_conv_b, ssm_dt_bias, ssm_a_log, ssm_d, ssm_norm,
           attn_q_norm, attn_k_norm, attn_f_bias, w_out, norm_ffn, router_group, router_group_bias,
           router_expert, router_expert_bias, moe_w_gate, moe_w_up, moe_w_down):
    nb, seq, _ = x_prompt.shape
    ns = x_sample.shape[0]
    depth = w_in.shape[0]
    n_pages = page_table.shape[1]
    tp = nb * seq
    tm_in, tm_out = 640, 320
    assert (tp + ns) % tm_in == 0 and (tp + ns) % tm_out == 0 and tp % ns == 0
    precise_in = _precise_tiles(tm_in, nb, seq, ns)
    precise_out = _precise_tiles(tm_out, nb, seq, ns)

    x = jnp.concatenate([x_prompt.reshape(tp, D_MODEL), x_sample.reshape(ns, D_MODEL)], axis=0)
    kt_pool = jnp.transpose(cache_k, (0, 1, 3, 4, 2))
    vt_pool = jnp.transpose(cache_v, (0, 1, 3, 4, 2))
    n_pool = cache_logf.shape[1]
    suf_rows, tot_rows = _logf_suffix(
        jnp.transpose(cache_logf, (0, 3, 1, 2)).reshape(depth * H_C * n_pool, PAGE_SIZE))
    suf_pool = suf_rows.reshape(depth, H_C, n_pool, 1, PAGE_SIZE)
    page_tot = tot_rows[:, 0].reshape(depth, H_C, n_pool)
    conv_lru = jnp.transpose(state_lru_conv, (0, 2, 1, 3))
    conv_ssm = jnp.transpose(state_ssm_conv, (0, 2, 1, 3))
    pt_flat = page_table.reshape(-1).astype(jnp.int32)

    outs_p, outs_s = [], []
    for l in range(depth):
        precise_tail = l < depth - 1
        wi_hi, wi_lo = _pack_w_in(w_in[l])
        u = _in_proj(x, norm_mix[l][None], wi_hi, wi_lo, precise_in, tm=tm_in)

        lcb, ba, bx, lam = lru_conv_b[l][None], lru_b_a[l][None], lru_b_x[l][None], lru_lambda[l][None]
        wa, wx = _block_diag(lru_w_a[l]), _block_diag(lru_w_x[l])
        scb = ssm_conv_b[l][None]
        dtb, alog = _pad_lanes(ssm_dt_bias[l]), _pad_lanes(ssm_a_log[l])
        ng = ssm_norm[l][None]
        gq = jnp.tile(attn_q_norm[l], H_C)[None]
        gk = jnp.tile(attn_k_norm[l], H_C)[None]
        fb = _pad_lanes(attn_f_bias[l])

        ya_p, lru_h_p, lru_conv_p = _lru_prompt(u, nb, seq, lru_conv_w[l], lcb, wa, ba, wx, bx, lam)
        yb_p, ssm_h_p, ssm_conv_p = _ssd_prompt(u, nb, seq, ssm_conv_w[l], scb, dtb, alog, _pad_lanes(ssm_d[l]), ng,
                                                precise_tail=precise_tail)
        k_p, v_p, lf_p, *attn_ops = _qk_prep_prompt(u, nb, seq, gq, gk, fb, with_lo=precise_tail)
        yc_p = _fox_prompt(attn_ops, nb, seq, tp + ns, precise_tail=precise_tail)

        (ya, lru_h_s, lru_conv_s, ssm_conv_s, dtxt, dat, bm, cm, xd, zg) = _sample_pre(
            u, tp, ns, state_lru[l], conv_lru[l], conv_ssm[l], lru_conv_w[l], lcb, wa, ba, wx, bx, lam,
            ssm_conv_w[l], scb, dtb, alog, jnp.repeat(ssm_d[l], HEAD_DIM)[None], ya_p)
        ssm_h_s, yb = _ssd_update(state_ssm, l, dtxt, dat, bm, cm, xd, zg, ng, yb_p, tp)
        q_s, k_s, lf_s = _qk_prep_sample(u, tp, ns, gq, gk, fb)
        v_s = u[tp:, OFF_V:OFF_V + D_C]
        tg = page_tot[l][:, page_table]
        later_pages = jnp.flip(jnp.cumsum(jnp.flip(tg, -1), -1), -1) - tg
        off = jnp.transpose(later_pages, (1, 2, 0)) + lf_s[:, None, 0:H_C]
        offsets = jnp.broadcast_to(off[..., None], (ns, n_pages, H_C, PAGE_SIZE))
        acc_t, ml_t = _paged_attn(pt_flat, q_s.T, kt_pool, vt_pool, suf_pool, offsets, l, ns, n_pages)
        yc = _sample_attn_post(acc_t, ml_t, q_s, k_s, v_s, yc_p, tp)
        wo_hi, wo_lo = _hi_lo(w_out[l])
        wr, rb = _pack_router(router_group[l], router_group_bias[l], router_expert[l], router_expert_bias[l])
        x1, h2, route = _out_proj(x, ya, yb, yc, wo_hi, wo_lo, norm_ffn[l][None], wr, rb, precise_out, tm=tm_out)

        src_tok, pos, tile_e, n_tiles = _moe_plan(route, MOE_TM)
        ys = _moe_gemm(tile_e, n_tiles, h2[src_tok], moe_w_gate, moe_w_up, moe_w_down, l)
        x = x1 + route[:, 2:3] * ys[pos[:, 0]] + route[:, 3:4] * ys[pos[:, 1]]
        if l < depth - 1:
            x = _moe_dense3(x1, tp, ns, norm_ffn[l][None], route, moe_w_gate, moe_w_up, moe_w_down, l, x)

        outs_p.append((
            k_p.reshape(nb, seq, H_C, HEAD_DIM), v_p.reshape(nb, seq, H_C, HEAD_DIM),
            lf_p[:, 0:H_C].reshape(nb, seq, H_C), lru_h_p[:, 0], lru_conv_p, ssm_h_p, ssm_conv_p))
        outs_s.append((
            k_s.reshape(ns, 1, H_C, HEAD_DIM), v_s.reshape(ns, 1, H_C, HEAD_DIM),
            lf_s[:, 0:H_C].reshape(ns, 1, H_C), lru_h_s, jnp.transpose(lru_conv_s, (1, 0, 2)),
            ssm_h_s, jnp.transpose(ssm_conv_s, (1, 0, 2))))

    stack = lambda states, j: jnp.stack([s[j] for s in states], axis=0)
    return (x[:tp].reshape(nb, seq, D_MODEL), x[tp:].reshape(ns, 1, D_MODEL),
            *[stack(outs_p, j) for j in range(7)], *[stack(outs_s, j) for j in range(7)])
```
